```python
import functools
import jax, jax.numpy as jnp
from jax import lax
import numpy as np

D_MODEL = 1024
BATCH = 8
SEQ = 2048
DEPTH = 2
DEC_BATCH = 32
DEC_SEQ = 4
PAST_LEN = 16384
PAGE_SIZE = 128

W_POOL = D_MODEL // 2
POOL_WINDOWS = (2, 4, 8, 16)
N_POOL_GROUPS = 4
POOL_GROUP_W = W_POOL // N_POOL_GROUPS
POOL_STATE = max(POOL_WINDOWS) - 1
W_SGU = D_MODEL // 2
SGU_GROUPS = 4
SGU_GROUP_W = W_SGU // SGU_GROUPS
CHUNK = 128
SB_HEADS = 8
SB_HEAD_DIM = 64
W_SB = SB_HEADS * SB_HEAD_DIM
Q_BLOCK = 128
SB_SCALE = SB_HEAD_DIM ** -0.5
SB_BIAS_HI = -4.0
SB_BIAS_LO = -10.0
W_CONV = D_MODEL // 2
CONV_WIDTH = 31
CONV_STATE = CONV_WIDTH - 1
N_BRANCH = 4
SPLIT_SIZES = (W_POOL, W_SGU, W_SGU, W_SB, W_SB, W_SB, W_CONV, W_CONV, N_BRANCH * D_MODEL)
IN_COLS = sum(SPLIT_SIZES)
N_MEM = 256
MEM_HEADS = 4
MEM_HEAD_DIM = D_MODEL // MEM_HEADS
MEM_SCALE = MEM_HEAD_DIM ** -0.5
N_GROUPS = 4
EXPERTS_PER_GROUP = 8
N_EXPERTS = N_GROUPS * EXPERTS_PER_GROUP
D_EXPERT = D_MODEL // 4
TOP_K_INNER = 2
EPS = 1e-6

kernel_name = 'hybrid_pool_sgu_stickbreak_conformer_hmoe_step'


def rmsnorm(x, g):
    xf = x.astype(jnp.float32)
    r = lax.rsqrt(jnp.mean(xf * xf, axis=-1, keepdims=True) + EPS)
    return (xf * r).astype(x.dtype) * g


def split_points():
    return [int(s) for s in np.cumsum(SPLIT_SIZES)[:-1]]


def pool_mix(a_ext, start_pos, w_map, scale):
    B, Lx, _ = a_ext.shape
    T = Lx - POOL_STATE
    gf = a_ext.reshape(B, Lx, N_POOL_GROUPS, POOL_GROUP_W).astype(jnp.float32)
    cs = jnp.concatenate([jnp.zeros_like(gf[:, :1]), jnp.cumsum(gf, axis=1)], axis=1)
    end = cs[:, POOL_STATE + 1:]
    pos = start_pos + jnp.arange(T)
    means = []
    for gi, w in enumerate(POOL_WINDOWS):
        begin = cs[:, POOL_STATE + 1 - w: POOL_STATE + 1 - w + T, gi]
        cnt = jnp.minimum(pos + 1, w).astype(jnp.float32)
        means.append((end[:, :, gi] - begin) / cnt[None, :, None])
    pooled = jnp.stack(means, axis=2)
    d = (pooled - gf[:, POOL_STATE:]).astype(a_ext.dtype)
    y = jnp.einsum('btgc,gcd->btgd', d, w_map)
    return y.reshape(B, T, W_POOL) * scale


def sgu_mix(u, v, g_v, w_s, b_s):
    B, T, _ = u.shape
    L = min(T, CHUNK)
    nc = T // L
    vn = rmsnorm(v, g_v)
    vg = vn.reshape(B, nc, L, SGU_GROUPS, SGU_GROUP_W)
    mask = jnp.tril(jnp.ones((L, L), dtype=bool))
    ws = jnp.where(mask[None], w_s[:, :L, :L], 0.0).astype(vg.dtype)
    s = jnp.einsum('gts,bnsgc->bntgc', ws, vg) + b_s[:, :L].T[None, None, :, :, None]
    return u * s.reshape(B, T, W_SGU), vn


def stick_break_weights(z, mask):
    log_beta = jax.nn.log_sigmoid(z)
    log_keep = jnp.where(mask, jax.nn.log_sigmoid(-z), 0.0)
    later = lax.cumsum(log_keep, axis=z.ndim - 1, reverse=True) - log_keep
    return jnp.where(mask, jnp.exp(log_beta + later), 0.0)


def sb_prompt(q, k, v, bias):
    B, T, H, Dh = q.shape
    nb = T // Q_BLOCK
    k_pos = jnp.arange(T)
    qb = q.reshape(B, nb, Q_BLOCK, H, Dh).transpose(1, 0, 2, 3, 4)
    pb = jnp.arange(T).reshape(nb, Q_BLOCK)
    bias_f = bias.astype(jnp.float32)[None, :, None, None]

    def one_block(args):
        qblk, qpos = args
        z = jnp.einsum('bqhd,bkhd->bhqk', qblk, k).astype(jnp.float32) * SB_SCALE + bias_f
        a = stick_break_weights(z, qpos[:, None] > k_pos[None, :])
        return jnp.einsum('bhqk,bkhd->bqhd', a.astype(v.dtype), v)

    o = lax.map(one_block, (qb, pb))
    return o.transpose(1, 0, 2, 3, 4).reshape(B, T, H, Dh)


def sb_sample(q, k, v, bias, k_past, v_past):
    T = q.shape[1]
    P = k_past.shape[1]
    q_pos = P + jnp.arange(T)
    k_pos = jnp.arange(P + T)
    z = jnp.concatenate([jnp.einsum('bqhd,bkhd->bhqk', q, k_past),
                         jnp.einsum('bqhd,bkhd->bhqk', q, k)], axis=-1).astype(jnp.float32) * SB_SCALE
    z = z + bias.astype(jnp.float32)[None, :, None, None]
    a = stick_break_weights(z, q_pos[:, None] > k_pos[None, :]).astype(v.dtype)
    return (jnp.einsum('bhqk,bkhd->bqhd', a[..., :P], v_past)
            + jnp.einsum('bhqk,bkhd->bqhd', a[..., P:], v))


def conv_module(c_ext, w_dw, b_dw, ln_g, ln_b, w_br):
    y = lax.conv_general_dilated(c_ext, w_dw[:, None, :].astype(c_ext.dtype), window_strides=(1,), padding='VALID',
                                 dimension_numbers=('NWC', 'WIO', 'NWC'), feature_group_count=c_ext.shape[-1]) + b_dw
    yf = y.astype(jnp.float32)
    mu = jnp.mean(yf, axis=-1, keepdims=True)
    var = jnp.mean(jnp.square(yf - mu), axis=-1, keepdims=True)
    yn = ((yf - mu) * lax.rsqrt(var + EPS)).astype(y.dtype) * ln_g + ln_b
    return jax.nn.silu(yn) @ w_br


def mem_attend(h, mem_k, mem_v, w_xq, w_xo):
    B, T, _ = h.shape
    q = (h @ w_xq).reshape(B, T, MEM_HEADS, MEM_HEAD_DIM)
    s = jnp.einsum('bthd,bmhd->bhtm', q, mem_k).astype(jnp.float32) * MEM_SCALE
    p = jax.nn.softmax(s, axis=-1).astype(h.dtype)
    o = jnp.einsum('bhtm,bmhd->bthd', p, mem_v).reshape(B, T, D_MODEL)
    return o @ w_xo


def hier_moe(h, w_rg, b_rg, w_re, b_re, w_e1, w_e3, w_e2):
    B, T, _ = h.shape
    p_grp = jax.nn.softmax((h @ w_rg + b_rg).astype(jnp.float32), axis=-1)
    g_idx = jnp.argmax(p_grp, axis=-1)
    g_w = jnp.max(p_grp, axis=-1)
    e_logits = (h @ w_re + b_re).astype(jnp.float32).reshape(B, T, N_GROUPS, EXPERTS_PER_GROUP)
    e_in = jnp.take_along_axis(e_logits, g_idx[..., None, None], axis=2)[..., 0, :]
    top_v, top_i = lax.top_k(e_in, TOP_K_INNER)
    top_w = jax.nn.softmax(top_v, axis=-1)
    inner = jnp.sum(top_w[..., None] * jax.nn.one_hot(top_i, EXPERTS_PER_GROUP, dtype=jnp.float32), axis=-2)
    comb = (g_w[..., None, None] * jax.nn.one_hot(g_idx, N_GROUPS, dtype=jnp.float32)[..., None]
            * inner[..., None, :]).reshape(B, T, N_EXPERTS).astype(h.dtype)
    out = jnp.zeros_like(h)
    for e in range(N_EXPERTS):
        hid = jax.nn.silu(h @ w_e1[e]) * (h @ w_e3[e])
        out = out + comb[..., e:e + 1] * (hid @ w_e2[e])
    return out


def trunk_layer(x, lp, pool_prev, conv_prev, start_pos, sb_fn, mem_k, mem_v):
    B, T, _ = x.shape
    h = rmsnorm(x, lp['g_mix'])
    proj = h @ lp['w_in']
    a_in, u, v, q, k, vv, glu_a, glu_b, gate_pre = jnp.split(proj, split_points(), axis=-1)
    a_ext = jnp.concatenate([pool_prev.astype(a_in.dtype), a_in], axis=1)
    y_a = pool_mix(a_ext, start_pos, lp['w_pool_map'], lp['pool_scale']) @ lp['w_br_pool']
    s_out, v_n = sgu_mix(u, v, lp['g_sgu_v'], lp['w_sgu_s'], lp['b_sgu'])
    y_b = s_out @ lp['w_br_sgu']
    kh = k.reshape(B, T, SB_HEADS, SB_HEAD_DIM)
    vh = vv.reshape(B, T, SB_HEADS, SB_HEAD_DIM)
    o_c = sb_fn(q.reshape(B, T, SB_HEADS, SB_HEAD_DIM), kh, vh, lp['sb_bias'])
    y_c = o_c.reshape(B, T, W_SB) @ lp['w_br_sb']
    c_in = glu_a * jax.nn.sigmoid(glu_b)
    c_ext = jnp.concatenate([conv_prev.astype(c_in.dtype), c_in], axis=1)
    y_d = conv_module(c_ext, lp['w_dw'], lp['b_dw'], lp['ln_g'], lp['ln_b'], lp['w_br_conv'])
    gates = jax.nn.sigmoid(gate_pre.reshape(B, T, N_BRANCH, D_MODEL) + lp['b_gate'])
    merged = gates[:, :, 0] * y_a + gates[:, :, 1] * y_b + gates[:, :, 2] * y_c + gates[:, :, 3] * y_d
    x = x + merged @ lp['w_out']
    x = x + mem_attend(rmsnorm(x, lp['g_xattn']), mem_k, mem_v, lp['w_xq'], lp['w_xo'])
    x = x + hier_moe(rmsnorm(x, lp['g_moe']), lp['w_rg'], lp['b_rg'], lp['w_re'], lp['b_re'],
                     lp['w_e1'], lp['w_e3'], lp['w_e2'])
    return x, a_ext[:, -POOL_STATE:], c_ext[:, -CONV_STATE:], kh, vh, v_n


def setup_inputs(seed: int = 0) -> dict:
    key = jax.random.key(seed)
    ks = iter(jax.random.split(key, 64))
    f32 = jnp.float32

    def nrm(shape, scale=1.0):
        return jax.random.normal(next(ks), shape, f32) * scale

    def gain(shape):
        return 1.0 + nrm(shape, 0.01)

    n_pages = PAST_LEN // PAGE_SIZE
    n_used = DEC_BATCH * n_pages
    n_phys = n_used + n_used // 4
    page_table = jax.random.permutation(next(ks), n_phys)[:n_used].reshape(DEC_BATCH, n_pages).astype(jnp.int32)
    Dm = D_MODEL
    return {
        'x_prompt': nrm((BATCH, SEQ, Dm)),
        'x_sample': nrm((DEC_BATCH, DEC_SEQ, Dm)),
        'mem_prompt': nrm((BATCH, N_MEM, Dm)),
        'cache_k': nrm((DEPTH, n_phys, PAGE_SIZE, SB_HEADS, SB_HEAD_DIM)),
        'cache_v': nrm((DEPTH, n_phys, PAGE_SIZE, SB_HEADS, SB_HEAD_DIM)),
        'cache_mem_k': nrm((DEPTH, DEC_BATCH, N_MEM, MEM_HEADS, MEM_HEAD_DIM)),
        'cache_mem_v': nrm((DEPTH, DEC_BATCH, N_MEM, MEM_HEADS, MEM_HEAD_DIM)),
        'state_pool': nrm((DEPTH, DEC_BATCH, POOL_STATE, W_POOL)),
        'state_conv': nrm((DEPTH, DEC_BATCH, CONV_STATE, W_CONV), 0.5),
        'page_table': page_table,
        'g_mix': gain((DEPTH, Dm)),
        'w_in': nrm((DEPTH, Dm, IN_COLS), Dm ** -0.5),
        'b_gate': nrm((DEPTH, N_BRANCH, Dm), 0.01),
        'w_pool_map': nrm((DEPTH, N_POOL_GROUPS, POOL_GROUP_W, POOL_GROUP_W), POOL_GROUP_W ** -0.5),
        'pool_scale': gain((DEPTH, W_POOL)),
        'w_br_pool': nrm((DEPTH, W_POOL, Dm), W_POOL ** -0.5),
        'g_sgu_v': gain((DEPTH, W_SGU)),
        'w_sgu_s': nrm((DEPTH, SGU_GROUPS, CHUNK, CHUNK), CHUNK ** -0.5),
        'b_sgu': gain((DEPTH, SGU_GROUPS, CHUNK)),
        'w_br_sgu': nrm((DEPTH, W_SGU, Dm), W_SGU ** -0.5),
        'sb_bias': jnp.linspace(SB_BIAS_HI, SB_BIAS_LO, SB_HEADS, dtype=f32)[None, :] + nrm((DEPTH, SB_HEADS), 0.1),
        'w_br_sb': nrm((DEPTH, W_SB, Dm), W_SB ** -0.5),
        'w_dw': nrm((DEPTH, CONV_WIDTH, W_CONV), CONV_WIDTH ** -0.5),
        'b_dw': nrm((DEPTH, W_CONV), 0.01),
        'ln_g': gain((DEPTH, W_CONV)),
        'ln_b': nrm((DEPTH, W_CONV), 0.01),
        'w_br_conv': nrm((DEPTH, W_CONV, Dm), W_CONV ** -0.5),
        'w_out': nrm((DEPTH, Dm, Dm), Dm ** -0.5),
        'g_xattn': gain((DEPTH, Dm)),
        'g_mem': gain((DEPTH, Dm)),
        'w_xq': nrm((DEPTH, Dm, Dm), Dm ** -0.5),
        'w_xk': nrm((DEPTH, Dm, Dm), Dm ** -0.5),
        'w_xv': nrm((DEPTH, Dm, Dm), Dm ** -0.5),
        'w_xo': nrm((DEPTH, Dm, Dm), Dm ** -0.5),
        'g_moe': gain((DEPTH, Dm)),
        'w_rg': nrm((DEPTH, Dm, N_GROUPS), Dm ** -0.5),
        'b_rg': nrm((DEPTH, N_GROUPS), 0.01),
        'w_re': nrm((DEPTH, Dm, N_EXPERTS), Dm ** -0.5),
        'b_re': nrm((DEPTH, N_EXPERTS), 0.01),
        'w_e1': nrm((DEPTH, N_EXPERTS, Dm, D_EXPERT), Dm ** -0.5),
        'w_e3': nrm((DEPTH, N_EXPERTS, Dm, D_EXPERT), Dm ** -0.5),
        'w_e2': nrm((DEPTH, N_EXPERTS, D_EXPERT, Dm), D_EXPERT ** -0.5),
        'g_final': gain((Dm,)),
    }


def reference(x_prompt, x_sample, mem_prompt, cache_k, cache_v, cache_mem_k, cache_mem_v, state_pool, state_conv,
              page_table, g_mix, w_in, b_gate, w_pool_map, pool_scale, w_br_pool, g_sgu_v, w_sgu_s, b_sgu, w_br_sgu,
              sb_bias, w_br_sb, w_dw, b_dw, ln_g, ln_b, w_br_conv, w_out, g_xattn, g_mem, w_xq, w_xk, w_xv, w_xo,
              g_moe, w_rg, b_rg, w_re, b_re, w_e1, w_e3, w_e2, g_final):
    B = x_prompt.shape[0]
    DB = x_sample.shape[0]
    past_len = page_table.shape[1] * cache_k.shape[2]
    xp, xs = x_prompt, x_sample
    kp_l, vp_l, ksm_l, vsm_l = [], [], [], []
    poolp_l, pools_l, convp_l, convs_l, sguv_l, memk_l, memv_l = [], [], [], [], [], [], []
    for l in range(DEPTH):
        lp = dict(g_mix=g_mix[l], w_in=w_in[l], b_gate=b_gate[l], w_pool_map=w_pool_map[l], pool_scale=pool_scale[l],
                  w_br_pool=w_br_pool[l], g_sgu_v=g_sgu_v[l], w_sgu_s=w_sgu_s[l], b_sgu=b_sgu[l], w_br_sgu=w_br_sgu[l],
                  sb_bias=sb_bias[l], w_br_sb=w_br_sb[l], w_dw=w_dw[l], b_dw=b_dw[l], ln_g=ln_g[l], ln_b=ln_b[l],
                  w_br_conv=w_br_conv[l], w_out=w_out[l], g_xattn=g_xattn[l], w_xq=w_xq[l], w_xo=w_xo[l],
                  g_moe=g_moe[l], w_rg=w_rg[l], b_rg=b_rg[l], w_re=w_re[l], b_re=b_re[l], w_e1=w_e1[l],
                  w_e3=w_e3[l], w_e2=w_e2[l])
        mem_n = rmsnorm(mem_prompt, g_mem[l])
        mk = (mem_n @ w_xk[l]).reshape(B, N_MEM, MEM_HEADS, MEM_HEAD_DIM)
        mv = (mem_n @ w_xv[l]).reshape(B, N_MEM, MEM_HEADS, MEM_HEAD_DIM)
        pool0 = jnp.zeros((B, POOL_STATE, W_POOL), xp.dtype)
        conv0 = jnp.zeros((B, CONV_STATE, W_CONV), xp.dtype)
        xp, pool_p, conv_p, k_p, v_p, _ = trunk_layer(xp, lp, pool0, conv0, 0, sb_prompt, mk, mv)
        k_past = cache_k[l][page_table].reshape(DB, past_len, SB_HEADS, SB_HEAD_DIM)
        v_past = cache_v[l][page_table].reshape(DB, past_len, SB_HEADS, SB_HEAD_DIM)
        sb_fn = functools.partial(sb_sample, k_past=k_past, v_past=v_past)
        xs, pool_s, conv_s, k_s, v_s, vn_s = trunk_layer(xs, lp, state_pool[l], state_conv[l], past_len, sb_fn,
                                                         cache_mem_k[l], cache_mem_v[l])
        kp_l.append(k_p); vp_l.append(v_p); ksm_l.append(k_s); vsm_l.append(v_s)
        poolp_l.append(pool_p); pools_l.append(pool_s); convp_l.append(conv_p); convs_l.append(conv_s)
        sguv_l.append(vn_s); memk_l.append(mk); memv_l.append(mv)
    y_prompt = rmsnorm(xp, g_final)
    y_sample = rmsnorm(xs, g_final)
    return (y_prompt, y_sample,
            jnp.stack(kp_l), jnp.stack(vp_l), jnp.stack(ksm_l), jnp.stack(vsm_l),
            jnp.stack(poolp_l), jnp.stack(pools_l), jnp.stack(convp_l), jnp.stack(convs_l),
            jnp.stack(sguv_l), jnp.stack(memk_l), jnp.stack(memv_l))
```

```python
import functools
import math

import jax
import jax.numpy as jnp
from jax import lax
from jax.experimental import pallas as pl
from jax.experimental.pallas import tpu as pltpu

F32 = jnp.float32
BF16 = jnp.bfloat16
EPS = 1e-6

LANES = 128
W_MIX = 512
POOL_WINDOWS = (2, 4, 8, 16)
POOL_STATE = max(POOL_WINDOWS) - 1
N_MIX_GROUPS = 4
MIX_GROUP_W = W_MIX // N_MIX_GROUPS
SGU_CHUNK = 128
SB_HEADS = 8
SB_HEAD_DIM = 64
SB_SCALE = SB_HEAD_DIM ** -0.5
SB_BLOCK = 128
SB_PAGES_PER_STEP = 8
SB_QROWS = 8
CONV_WIDTH = 31
CONV_STATE = CONV_WIDTH - 1
HALO = 32
CONV_ROWS = 32
MEM_HEADS = 4
N_GROUPS = 4
EXPERTS_PER_GROUP = 8
N_EXPERTS = N_GROUPS * EXPERTS_PER_GROUP
VMEM_LIMIT = 56 * 1024 * 1024


def _params(*sem):
    return pltpu.CompilerParams(dimension_semantics=sem, vmem_limit_bytes=VMEM_LIMIT)


def _rms(x, g):
    r = lax.rsqrt(jnp.mean(x * x, axis=-1, keepdims=True) + EPS)
    return (x * r) * g


def _sigmoid(x):
    return 1.0 / (1.0 + jnp.exp(-x))


def _log2(n):
    assert n & (n - 1) == 0
    return n.bit_length() - 1


def _dot(a, b):
    return jnp.dot(a, b, preferred_element_type=F32)


def _dot_nt(a, b):
    return lax.dot_general(a, b, (((1,), (1,)), ((), ())), preferred_element_type=F32)


def _norm_matmul_kernel(x_ref, g_ref, w_ref, o_ref, h_ref):
    @pl.when(pl.program_id(1) == 0)
    def _():
        h_ref[...] = _rms(x_ref[...], g_ref[...]).astype(BF16)

    o_ref[...] = _dot(h_ref[...], w_ref[...])


def norm_matmul(x, g, w, tm, tn):
    M, D = x.shape
    N = w.shape[1]
    return pl.pallas_call(
        _norm_matmul_kernel,
        grid=(M // tm, N // tn),
        in_specs=[
            pl.BlockSpec((tm, D), lambda i, j: (i, 0)),
            pl.BlockSpec((1, D), lambda i, j: (0, 0)),
            pl.BlockSpec((D, tn), lambda i, j: (0, j)),
        ],
        out_specs=pl.BlockSpec((tm, tn), lambda i, j: (i, j)),
        out_shape=jax.ShapeDtypeStruct((M, N), F32),
        scratch_shapes=[pltpu.VMEM((tm, D), BF16)],
        compiler_params=_params("parallel", "arbitrary"),
        name="norm_matmul",
    )(x, g.reshape(1, D), w)


def _matmul_residual_kernel(x_ref, a_ref, w_ref, o_ref):
    o_ref[...] = x_ref[...] + _dot(a_ref[...].astype(BF16), w_ref[...])


def matmul_residual(x, a, w, tm):
    M, N = x.shape
    K = a.shape[1]
    return pl.pallas_call(
        _matmul_residual_kernel,
        grid=(M // tm,),
        in_specs=[
            pl.BlockSpec((tm, N), lambda i: (i, 0)),
            pl.BlockSpec((tm, K), lambda i: (i, 0)),
            pl.BlockSpec((K, N), lambda i: (0, 0)),
        ],
        out_specs=pl.BlockSpec((tm, N), lambda i: (i, 0)),
        out_shape=jax.ShapeDtypeStruct((M, N), F32),
        compiler_params=_params("parallel"),
        name="matmul_residual",
    )(x, a, w)


def _mixers_kernel(a_ref, u_ref, v_ref, ga_ref, gb_ref, pool_prev_ref, conv_prev_ref,
                   wmap_ref, pscale_ref, gv_ref, ws_ref, bs_ref, wdw_ref, bdw_ref, lng_ref, lnb_ref,
                   ya_ref, yb_ref, yd_ref, vn_ref, pool_state_ref, conv_state_ref,
                   aext_ref, cext_ref, *, tt, start_pos, chunk):
    t = pl.program_id(1)
    nt = pl.num_programs(1)

    @pl.when(t == 0)
    def _():
        aext_ref[pl.ds(HALO - POOL_STATE, POOL_STATE), :] = pool_prev_ref[0]
        cext_ref[pl.ds(HALO - CONV_STATE, CONV_STATE), :] = conv_prev_ref[0]

    @pl.when(t > 0)
    def _():
        aext_ref[pl.ds(HALO - POOL_STATE, POOL_STATE), :] = aext_ref[pl.ds(HALO + tt - POOL_STATE, POOL_STATE), :]
        cext_ref[pl.ds(HALO - CONV_STATE, CONV_STATE), :] = cext_ref[pl.ds(HALO + tt - CONV_STATE, CONV_STATE), :]

    a_in = a_ref[0]
    aext_ref[pl.ds(HALO, tt), :] = a_in
    c_in = ga_ref[0] * _sigmoid(gb_ref[0])
    cext_ref[pl.ds(HALO, tt), :] = c_in

    pos = start_pos + t * tt + lax.broadcasted_iota(jnp.int32, (tt, 1), 0)
    for gi, w in enumerate(POOL_WINDOWS):
        cols = slice(gi * MIX_GROUP_W, (gi + 1) * MIX_GROUP_W)
        s = a_in[:, cols]
        for j in range(1, w):
            s = s + aext_ref[pl.ds(HALO - j, tt), cols]
        cnt = jnp.minimum(pos + 1, w).astype(F32)
        d = s / cnt - a_in[:, cols]
        y = _dot(d.astype(BF16), wmap_ref[gi])
        ya_ref[0, :, cols] = (y * pscale_ref[:, cols]).astype(ya_ref.dtype)

    vn = _rms(v_ref[0], gv_ref[...])
    vn_ref[0] = vn
    u = u_ref[0]
    row = lax.broadcasted_iota(jnp.int32, (chunk, chunk), 0)
    col = lax.broadcasted_iota(jnp.int32, (chunk, chunk), 1)
    for gi in range(N_MIX_GROUPS):
        cols = slice(gi * MIX_GROUP_W, (gi + 1) * MIX_GROUP_W)
        ws = jnp.where(row >= col, ws_ref[gi][:chunk, :chunk], 0.0)
        b_col = bs_ref[:, gi:gi + 1]
        for c in range(tt // chunk):
            rows = slice(c * chunk, (c + 1) * chunk)
            vg = vn[rows, cols]
            if chunk == SGU_CHUNK:
                sg = _dot(ws.astype(BF16), vg.astype(BF16))
            else:
                sg = jnp.zeros((chunk, MIX_GROUP_W), F32)
                for k in range(chunk):
                    sg = sg + ws[:, k:k + 1] * vg[k:k + 1, :]
            yb_ref[0, rows, cols] = (u[rows, cols] * (sg + b_col)).astype(yb_ref.dtype)

    rb = min(tt, CONV_ROWS)
    for r0 in range(0, tt, rb):
        acc = jnp.zeros((rb, W_MIX), F32) + bdw_ref[...]
        for j in range(CONV_WIDTH):
            acc = acc + wdw_ref[j:j + 1, :] * cext_ref[pl.ds(HALO - CONV_STATE + r0 + j, rb), :]
        mu = jnp.mean(acc, axis=-1, keepdims=True)
        yc = acc - mu
        var = jnp.mean(yc * yc, axis=-1, keepdims=True)
        yn = (yc * lax.rsqrt(var + EPS)) * lng_ref[...] + lnb_ref[...]
        yd_ref[0, pl.ds(r0, rb), :] = (yn * _sigmoid(yn)).astype(yd_ref.dtype)

    @pl.when(t == nt - 1)
    def _():
        pool_state_ref[0] = aext_ref[pl.ds(HALO + tt - POOL_STATE, POOL_STATE), :]
        conv_state_ref[0] = cext_ref[pl.ds(HALO + tt - CONV_STATE, CONV_STATE), :]


def mixers(proj, pool_prev, conv_prev, wmap, pscale, gv, ws, bs, wdw, bdw, lng, lnb, *, tt, start_pos, act_dtype):
    B, T, _ = proj.shape
    chunk = min(T, SGU_CHUNK)
    bs_t = bs[:, :chunk].T
    col_block = lambda c: pl.BlockSpec((1, tt, W_MIX), lambda b, t: (b, t, c))
    whole = lambda a: pl.BlockSpec(a.shape, lambda b, t: (0,) * a.ndim)
    row2d = lambda a: a.reshape(1, -1)
    pscale, gv, bdw, lng, lnb = map(row2d, (pscale, gv, bdw, lng, lnb))
    act = jax.ShapeDtypeStruct((B, T, W_MIX), act_dtype)
    tile = pl.BlockSpec((1, tt, W_MIX), lambda b, t: (b, t, 0))
    return pl.pallas_call(
        functools.partial(_mixers_kernel, tt=tt, start_pos=start_pos, chunk=chunk),
        grid=(B, T // tt),
        in_specs=[col_block(0), col_block(1), col_block(2), col_block(6), col_block(7),
                  pl.BlockSpec((1, POOL_STATE, W_MIX), lambda b, t: (b, 0, 0)),
                  pl.BlockSpec((1, CONV_STATE, W_MIX), lambda b, t: (b, 0, 0)),
                  whole(wmap), whole(pscale), whole(gv), whole(ws), whole(bs_t), whole(wdw), whole(bdw),
                  whole(lng), whole(lnb)],
        out_specs=[tile, tile, tile, tile,
                   pl.BlockSpec((1, POOL_STATE, W_MIX), lambda b, t: (b, 0, 0)),
                   pl.BlockSpec((1, CONV_STATE, W_MIX), lambda b, t: (b, 0, 0))],
        out_shape=[act, act, act, jax.ShapeDtypeStruct((B, T, W_MIX), F32),
                   jax.ShapeDtypeStruct((B, POOL_STATE, W_MIX), F32),
                   jax.ShapeDtypeStruct((B, CONV_STATE, W_MIX), F32)],
        scratch_shapes=[pltpu.VMEM((HALO + tt, W_MIX), F32), pltpu.VMEM((HALO + tt, W_MIX), F32)],
        compiler_params=_params("parallel", "arbitrary"),
        name="mixers",
    )(proj, proj, proj, proj, proj, pool_prev, conv_prev, wmap, pscale, gv, ws, bs_t, wdw, bdw, lng, lnb)


def _suffix_matrix():
    row = lax.broadcasted_iota(jnp.int32, (SB_BLOCK, 2 * SB_BLOCK), 0)
    col = lax.broadcasted_iota(jnp.int32, (SB_BLOCK, 2 * SB_BLOCK), 1)
    return jnp.where((row > col) | (col >= SB_BLOCK), 1.0, 0.0).astype(BF16)


def _stick_block(z, suffix, carry, mask):
    l1p = jnp.log1p(jnp.exp(-jnp.abs(z)))
    log_beta = jnp.minimum(z, 0.0) - l1p
    log_keep = -jnp.maximum(z, 0.0) - l1p
    if mask is not None:
        log_keep = jnp.where(mask, log_keep, 0.0)
    hi = log_keep.astype(BF16)
    lo = (log_keep - hi.astype(F32)).astype(BF16)
    sums = _dot(hi, suffix) + _dot(lo, suffix)
    a = jnp.exp(log_beta + sums[:, :SB_BLOCK] + carry)
    if mask is not None:
        a = jnp.where(mask, a, 0.0)
    return a, carry + sums[:, SB_BLOCK:]


def _sb_prompt_kernel(bias_ref, q_ref, k_ref, v_ref, o_ref):
    qi = pl.program_id(1)
    suffix = _suffix_matrix()
    row = lax.broadcasted_iota(jnp.int32, (SB_BLOCK, SB_BLOCK), 0)
    col = lax.broadcasted_iota(jnp.int32, (SB_BLOCK, SB_BLOCK), 1)
    causal = row > col
    for h in range(SB_HEADS):
        cols = slice(h * SB_HEAD_DIM, (h + 1) * SB_HEAD_DIM)
        q = (q_ref[0, :, cols] * SB_SCALE).astype(BF16)
        bias = bias_ref[h]

        def block(kb, carry, acc, mask):
            start = pl.multiple_of(kb * SB_BLOCK, SB_BLOCK)
            k = k_ref[0, pl.ds(start, SB_BLOCK), cols].astype(BF16)
            v = v_ref[0, pl.ds(start, SB_BLOCK), cols].astype(BF16)
            a, carry = _stick_block(_dot_nt(q, k) + bias, suffix, carry, mask)
            return carry, acc + _dot(a.astype(BF16), v)

        carry = jnp.zeros((SB_BLOCK, SB_BLOCK), F32)
        acc = jnp.zeros((SB_BLOCK, SB_HEAD_DIM), F32)
        carry, acc = block(qi, carry, acc, causal)

        def body(it, ca):
            return block(qi - 1 - it, ca[0], ca[1], None)

        carry, acc = lax.fori_loop(0, qi, body, (carry, acc))
        o_ref[0, :, cols] = acc.astype(o_ref.dtype)


def sb_prompt(proj, bias, act_dtype):
    B, T, _ = proj.shape
    return pl.pallas_call(
        _sb_prompt_kernel,
        grid=(B, T // SB_BLOCK),
        in_specs=[pl.BlockSpec(memory_space=pltpu.SMEM),
                  pl.BlockSpec((1, SB_BLOCK, W_MIX), lambda b, i: (b, i, 3)),
                  pl.BlockSpec((1, T, W_MIX), lambda b, i: (b, 0, 4)),
                  pl.BlockSpec((1, T, W_MIX), lambda b, i: (b, 0, 5))],
        out_specs=pl.BlockSpec((1, SB_BLOCK, W_MIX), lambda b, i: (b, i, 0)),
        out_shape=jax.ShapeDtypeStruct((B, T, W_MIX), act_dtype),
        compiler_params=_params("parallel", "arbitrary"),
        name="sb_prompt",
    )(bias, proj, proj, proj)


def _sb_sample_kernel(pt_ref, q_ref, bias_ref, knew_ref, vnew_ref, *refs, pages_per_step):
    k_refs = refs[:pages_per_step]
    v_refs = refs[pages_per_step:2 * pages_per_step]
    o_ref, carry_ref, acc_ref = refs[2 * pages_per_step:]
    s = pl.program_id(1)
    suffix = _suffix_matrix()
    bias = bias_ref[...]
    head_rows = [slice(h * SB_QROWS, (h + 1) * SB_QROWS) for h in range(SB_HEADS)]
    qs = [q_ref[0, rows, :].astype(BF16) for rows in head_rows]

    def page(k_ref, v_ref, mask):
        z = jnp.concatenate([_dot_nt(qs[h], k_ref[0, 0, :, h, :].astype(BF16)) for h in range(SB_HEADS)], axis=0)
        a, carry = _stick_block(z + bias, suffix, carry_ref[...], mask)
        carry_ref[...] = carry
        for h, rows in enumerate(head_rows):
            acc_ref[rows, :] += _dot(a[rows].astype(BF16), v_ref[0, 0, :, h, :].astype(BF16))

    @pl.when(s == 0)
    def _():
        carry_ref[...] = jnp.zeros_like(carry_ref)
        acc_ref[...] = jnp.zeros_like(acc_ref)
        qidx = lax.broadcasted_iota(jnp.int32, carry_ref.shape, 0) & (SB_QROWS - 1)
        kidx = lax.broadcasted_iota(jnp.int32, carry_ref.shape, 1)
        page(knew_ref, vnew_ref, qidx > kidx)

    for j in range(pages_per_step):
        page(k_refs[j], v_refs[j], None)

    @pl.when(s == pl.num_programs(1) - 1)
    def _():
        o_ref[0] = acc_ref[...]


def sb_sample(q, k_new, v_new, bias, cache_k, cache_v, layer, page_table):
    DB, t_new, _ = q.shape
    assert t_new <= SB_QROWS
    n_pages = page_table.shape[1]
    pages_per_step = math.gcd(SB_PAGES_PER_STEP, n_pages)
    rows = SB_HEADS * SB_QROWS
    heads = lambda a: a.reshape(DB, t_new, SB_HEADS, SB_HEAD_DIM)
    q_hm = jnp.pad(heads(q * SB_SCALE).transpose(0, 2, 1, 3), ((0, 0), (0, 0), (0, SB_QROWS - t_new), (0, 0)))
    q_hm = q_hm.reshape(DB, rows, SB_HEAD_DIM)
    bias_col = jnp.broadcast_to(jnp.repeat(bias, SB_QROWS)[:, None], (rows, SB_BLOCK)).astype(F32)
    as_page = lambda a: jnp.pad(heads(a), ((0, 0), (0, SB_BLOCK - t_new), (0, 0), (0, 0)))[:, None]

    def page_spec(j):
        def index(b, s, pt):
            return (layer, pt[b, n_pages - 1 - (s * pages_per_step + j)], 0, 0, 0)
        return pl.BlockSpec((1, 1, SB_BLOCK, SB_HEADS, SB_HEAD_DIM), index)

    per_seq = lambda shape: pl.BlockSpec((1,) + shape, lambda b, s, pt: (b,) + (0,) * len(shape))
    new_page = per_seq((1, SB_BLOCK, SB_HEADS, SB_HEAD_DIM))
    grid_spec = pltpu.PrefetchScalarGridSpec(
        num_scalar_prefetch=1,
        grid=(DB, n_pages // pages_per_step),
        in_specs=[per_seq((rows, SB_HEAD_DIM)), pl.BlockSpec((rows, SB_BLOCK), lambda b, s, pt: (0, 0)), new_page, new_page]
        + [page_spec(j) for j in range(pages_per_step)] * 2,
        out_specs=per_seq((rows, SB_HEAD_DIM)),
        scratch_shapes=[pltpu.VMEM((rows, SB_BLOCK), F32), pltpu.VMEM((rows, SB_HEAD_DIM), F32)],
    )
    o = pl.pallas_call(
        functools.partial(_sb_sample_kernel, pages_per_step=pages_per_step),
        grid_spec=grid_spec,
        out_shape=jax.ShapeDtypeStruct((DB, rows, SB_HEAD_DIM), F32),
        compiler_params=_params("parallel", "arbitrary"),
        name="sb_sample",
    )(page_table, q_hm, bias_col, as_page(k_new), as_page(v_new), *([cache_k] * pages_per_step),
      *([cache_v] * pages_per_step))
    o = o.reshape(DB, SB_HEADS, SB_QROWS, SB_HEAD_DIM)[:, :, :t_new]
    return o.transpose(0, 2, 1, 3).reshape(DB, t_new, W_MIX)


def _merge_kernel(x_ref, ya_ref, yb_ref, yc_ref, yd_ref, gate_ref, bg_ref, wbr_ref, wout_ref, o_ref):
    D = x_ref.shape[-1]
    merged = jnp.zeros(x_ref.shape, F32)
    for i, y_ref in enumerate((ya_ref, yb_ref, yc_ref, yd_ref)):
        gate = _sigmoid(gate_ref[:, i * D:(i + 1) * D] + bg_ref[i:i + 1, :])
        merged = merged + gate * _dot(y_ref[...].astype(BF16), wbr_ref[i])
    o_ref[...] = x_ref[...] + _dot(merged.astype(BF16), wout_ref[...])


def merge_out(x, ya, yb, yc, yd, proj, b_gate, w_br, w_out, tm):
    M, D = x.shape
    act = pl.BlockSpec((tm, W_MIX), lambda i: (i, 0))
    whole = lambda a: pl.BlockSpec(a.shape, lambda i: (0,) * a.ndim)
    return pl.pallas_call(
        _merge_kernel,
        grid=(M // tm,),
        in_specs=[pl.BlockSpec((tm, D), lambda i: (i, 0)), act, act, act, act,
                  pl.BlockSpec((tm, 4 * D), lambda i: (i, 1)), whole(b_gate), whole(w_br), whole(w_out)],
        out_specs=pl.BlockSpec((tm, D), lambda i: (i, 0)),
        out_shape=jax.ShapeDtypeStruct((M, D), F32),
        compiler_params=_params("parallel"),
        name="merge_out",
    )(x, ya, yb, yc, yd, proj, b_gate, w_br, w_out)


def _mem_attn_kernel(q_ref, k_ref, v_ref, o_ref):
    dh = q_ref.shape[-1] // MEM_HEADS
    scale = dh ** -0.5
    for h in range(MEM_HEADS):
        cols = slice(h * dh, (h + 1) * dh)
        s = _dot_nt(q_ref[0, :, cols].astype(BF16), k_ref[0, :, cols].astype(BF16)) * scale
        e = jnp.exp(s - jnp.max(s, axis=-1, keepdims=True))
        p = e / jnp.sum(e, axis=-1, keepdims=True)
        o_ref[0, :, cols] = _dot(p.astype(BF16), v_ref[0, :, cols].astype(BF16)).astype(o_ref.dtype)


def mem_attn(q, mem_k, mem_v, tt, act_dtype):
    B, T, D = q.shape
    n_mem = mem_k.shape[1]
    mem = pl.BlockSpec((1, n_mem, D), lambda b, t: (b, 0, 0))
    return pl.pallas_call(
        _mem_attn_kernel,
        grid=(B, T // tt),
        in_specs=[pl.BlockSpec((1, tt, D), lambda b, t: (b, t, 0)), mem, mem],
        out_specs=pl.BlockSpec((1, tt, D), lambda b, t: (b, t, 0)),
        out_shape=jax.ShapeDtypeStruct((B, T, D), act_dtype),
        compiler_params=_params("parallel", "arbitrary"),
        name="mem_attn",
    )(q, mem_k, mem_v)


def _route(logits):
    lane_i = lax.broadcasted_iota(jnp.int32, logits.shape, 1)
    lane = lane_i.astype(F32)
    neg = -jnp.inf
    big = float(LANES)
    lg = jnp.where(lane_i < N_GROUPS, logits, neg)
    g_max = jnp.max(lg, axis=-1, keepdims=True)
    g_w = 1.0 / jnp.sum(jnp.exp(lg - g_max), axis=-1, keepdims=True)
    g_idx = jnp.min(jnp.where(lg == g_max, lane, big), axis=-1, keepdims=True)
    e_lane = lane_i - N_GROUPS
    e_group = (e_lane >> _log2(EXPERTS_PER_GROUP)).astype(F32)
    in_group = (e_lane >= 0) & (e_lane < N_EXPERTS) & (e_group == g_idx)
    le = jnp.where(in_group, logits, neg)
    v1 = jnp.max(le, axis=-1, keepdims=True)
    i1 = jnp.min(jnp.where(in_group & (le == v1), lane, big), axis=-1, keepdims=True)
    rest = in_group & (lane != i1)
    le2 = jnp.where(rest, logits, neg)
    v2 = jnp.max(le2, axis=-1, keepdims=True)
    i2 = jnp.min(jnp.where(rest & (le2 == v2), lane, big), axis=-1, keepdims=True)
    e2 = jnp.exp(v2 - v1)
    w1 = 1.0 / (1.0 + e2)
    w2 = e2 / (1.0 + e2)
    return g_w * (jnp.where(lane == i1, w1, 0.0) + jnp.where(lane == i2, w2, 0.0))


def _moe_kernel(x_ref, g_ref, wr_ref, br_ref, w13_ref, w2_ref, gf_ref, o_ref, h_ref, comb_ref, acc_ref, *, final_norm):
    e = pl.program_id(1)

    @pl.when(e == 0)
    def _():
        h = _rms(x_ref[...], g_ref[...])
        h_ref[...] = h.astype(BF16)
        comb_ref[...] = _route(_dot(h.astype(BF16), wr_ref[...]) + br_ref[...])
        acc_ref[...] = jnp.zeros_like(acc_ref)

    de = w2_ref.shape[1]
    lane = lax.broadcasted_iota(jnp.int32, comb_ref.shape, 1)
    weight = jnp.sum(jnp.where(lane == e + N_GROUPS, comb_ref[...], 0.0), axis=-1, keepdims=True)
    hid13 = _dot(h_ref[...], w13_ref[0])
    g1 = hid13[:, :de]
    hid = (g1 * _sigmoid(g1)) * hid13[:, de:]
    acc_ref[...] += weight * _dot(hid.astype(BF16), w2_ref[0])

    @pl.when(e == pl.num_programs(1) - 1)
    def _():
        y = x_ref[...] + acc_ref[...]
        o_ref[...] = _rms(y, gf_ref[...]) if final_norm else y


def moe(x, g, w_router, b_router, w13, w2, g_final, tm, final_norm):
    M, D = x.shape
    de = w2.shape[1]
    vec = lambda a: pl.BlockSpec(a.shape, lambda i, e: (0,) * a.ndim)
    g, g_final = g.reshape(1, D), g_final.reshape(1, D)
    return pl.pallas_call(
        functools.partial(_moe_kernel, final_norm=final_norm),
        grid=(M // tm, N_EXPERTS),
        in_specs=[pl.BlockSpec((tm, D), lambda i, e: (i, 0)), vec(g), vec(w_router), vec(b_router),
                  pl.BlockSpec((1, D, 2 * de), lambda i, e: (e, 0, 0)),
                  pl.BlockSpec((1, de, D), lambda i, e: (e, 0, 0)), vec(g_final)],
        out_specs=pl.BlockSpec((tm, D), lambda i, e: (i, 0)),
        out_shape=jax.ShapeDtypeStruct((M, D), F32),
        scratch_shapes=[pltpu.VMEM((tm, D), BF16), pltpu.VMEM((tm, LANES), F32), pltpu.VMEM((tm, D), F32)],
        compiler_params=_params("parallel", "arbitrary"),
        name="moe",
    )(x, g, w_router, b_router, w13, w2, g_final)


def _layer(x, lw, pool_prev, conv_prev, mem_k, mem_v, sb_fn, *, start_pos, tiles, act_dtype, final_norm):
    B, T, D = x.shape
    M = B * T
    x2 = x.reshape(M, D)
    proj = norm_matmul(x2, lw["g_mix"], lw["w_in"], tiles["tm_in"], tiles["tn_in"])
    proj3 = proj.reshape(B, T, -1)
    ya, yb, yd, vn, pool_state, conv_state = mixers(
        proj3, pool_prev, conv_prev, lw["w_pool_map"], lw["pool_scale"], lw["g_sgu_v"], lw["w_sgu_s"], lw["b_sgu"],
        lw["w_dw"], lw["b_dw"], lw["ln_g"], lw["ln_b"], tt=tiles["tt_mix"], start_pos=start_pos, act_dtype=act_dtype)
    k_new = proj3[:, :, 4 * W_MIX:5 * W_MIX]
    v_new = proj3[:, :, 5 * W_MIX:6 * W_MIX]
    yc = sb_fn(proj3, k_new, v_new)
    flat = lambda a: a.reshape(M, W_MIX)
    x2 = merge_out(x2, flat(ya), flat(yb), flat(yc), flat(yd), proj, lw["b_gate"], lw["w_br"], lw["w_out"], tiles["tm_merge"])
    q = norm_matmul(x2, lw["g_xattn"], lw["w_xq"], tiles["tm_in"], tiles["tn_in"])
    o = mem_attn(q.reshape(B, T, D), mem_k, mem_v, tiles["tt_mem"], act_dtype)
    x2 = matmul_residual(x2, o.reshape(M, D), lw["w_xo"], tiles["tm_merge"])
    x2 = moe(x2, lw["g_moe"], lw["w_router"], lw["b_router"], lw["w_e13"], lw["w_e2"], lw["g_final"], tiles["tm_moe"],
             final_norm)
    heads = lambda a: a.reshape(B, T, SB_HEADS, SB_HEAD_DIM)
    return x2.reshape(B, T, D), pool_state, conv_state, heads(k_new), heads(v_new), vn


def kernel(x_prompt, x_sample, mem_prompt, cache_k, cache_v, cache_mem_k, cache_mem_v, state_pool, state_conv,
           page_table, g_mix, w_in, b_gate, w_pool_map, pool_scale, w_br_pool, g_sgu_v, w_sgu_s, b_sgu, w_br_sgu,
           sb_bias, w_br_sb, w_dw, b_dw, ln_g, ln_b, w_br_conv, w_out, g_xattn, g_mem, w_xq, w_xk, w_xv, w_xo,
           g_moe, w_rg, b_rg, w_re, b_re, w_e1, w_e3, w_e2, g_final):
    B, T, D = x_prompt.shape
    DB, TS, _ = x_sample.shape
    depth = w_in.shape[0]
    n_mem = mem_prompt.shape[1]
    past_len = page_table.shape[1] * cache_k.shape[2]
    pad = LANES - N_GROUPS - N_EXPERTS

    prompt_tiles = dict(tm_in=min(B * T, 1024), tn_in=1024, tt_mix=min(T, 256), tm_merge=min(B * T, 256),
                        tt_mem=min(T, 512), tm_moe=min(B * T, 1024))
    sample_tiles = dict(tm_in=DB * TS, tn_in=1024, tt_mix=TS, tm_merge=DB * TS, tt_mem=TS, tm_moe=DB * TS)

    xp, xs = x_prompt, x_sample
    outs = [[] for _ in range(11)]
    for l in range(depth):
        lw = dict(
            g_mix=g_mix[l], w_in=w_in[l].astype(BF16), b_gate=b_gate[l], w_pool_map=w_pool_map[l].astype(BF16),
            pool_scale=pool_scale[l], g_sgu_v=g_sgu_v[l], w_sgu_s=w_sgu_s[l], b_sgu=b_sgu[l],
            w_dw=w_dw[l], b_dw=b_dw[l], ln_g=ln_g[l], ln_b=ln_b[l],
            w_br=jnp.stack([w_br_pool[l], w_br_sgu[l], w_br_sb[l], w_br_conv[l]]).astype(BF16),
            w_out=w_out[l].astype(BF16), g_xattn=g_xattn[l], w_xq=w_xq[l].astype(BF16), w_xo=w_xo[l].astype(BF16),
            g_moe=g_moe[l],
            w_router=jnp.pad(jnp.concatenate([w_rg[l], w_re[l]], axis=1), ((0, 0), (0, pad))).astype(BF16),
            b_router=jnp.pad(jnp.concatenate([b_rg[l], b_re[l]]), (0, pad)).reshape(1, LANES),
            w_e13=jnp.concatenate([w_e1[l], w_e3[l]], axis=-1).astype(BF16), w_e2=w_e2[l].astype(BF16),
            g_final=g_final)
        final_norm = l == depth - 1

        w_kv = jnp.concatenate([w_xk[l], w_xv[l]], axis=1).astype(BF16)
        mkv = norm_matmul(mem_prompt.reshape(B * n_mem, D), g_mem[l], w_kv, min(B * n_mem, 1024), 1024)
        mk = mkv[:, :D].reshape(B, n_mem, D)
        mv = mkv[:, D:].reshape(B, n_mem, D)
        sb_p = lambda proj3, k_new, v_new: sb_prompt(proj3, sb_bias[l], BF16)
        xp, pool_p, conv_p, k_p, v_p, _ = _layer(
            xp, lw, jnp.zeros((B, POOL_STATE, W_MIX), F32), jnp.zeros((B, CONV_STATE, W_MIX), F32), mk, mv, sb_p,
            start_pos=0, tiles=prompt_tiles, act_dtype=BF16, final_norm=final_norm)

        sb_s = lambda proj3, k_new, v_new: sb_sample(
            proj3[:, :, 3 * W_MIX:4 * W_MIX], k_new, v_new, sb_bias[l], cache_k, cache_v, l, page_table)
        xs, pool_s, conv_s, k_s, v_s, vn_s = _layer(
            xs, lw, state_pool[l], state_conv[l], cache_mem_k[l].reshape(DB, n_mem, D),
            cache_mem_v[l].reshape(DB, n_mem, D), sb_s, start_pos=past_len, tiles=sample_tiles, act_dtype=F32,
            final_norm=final_norm)

        heads = lambda a: a.reshape(B, n_mem, MEM_HEADS, D // MEM_HEADS)
        for lst, val in zip(outs, (k_p, v_p, k_s, v_s, pool_p, pool_s, conv_p, conv_s, vn_s, heads(mk), heads(mv))):
            lst.append(val)
    return (xp, xs) + tuple(jnp.stack(o) for o in outs)
```

```python
import functools
import math

import jax
import jax.numpy as jnp
from jax import lax
from jax.experimental import pallas as pl
from jax.experimental.pallas import tpu as pltpu

F32 = jnp.float32
BF16 = jnp.bfloat16
EPS = 1e-6

LANES = 128
W_MIX = 512
POOL_WINDOWS = (2, 4, 8, 16)
POOL_STATE = max(POOL_WINDOWS) - 1
N_MIX_GROUPS = 4
MIX_GROUP_W = W_MIX // N_MIX_GROUPS
SGU_CHUNK = 128
SB_HEADS = 8
SB_HEAD_DIM = 64
SB_SCALE = SB_HEAD_DIM ** -0.5
SB_BLOCK = 128
SB_PAGES_PER_STEP = 8
SB_QROWS = 8
CONV_WIDTH = 31
CONV_STATE = CONV_WIDTH - 1
HALO = 32
CONV_ROWS = 32
MEM_HEADS = 4
N_GROUPS = 4
EXPERTS_PER_GROUP = 8
N_EXPERTS = N_GROUPS * EXPERTS_PER_GROUP
VMEM_LIMIT = 56 * 1024 * 1024


def _params(*sem):
    return pltpu.CompilerParams(dimension_semantics=sem, vmem_limit_bytes=VMEM_LIMIT)


def _rms(x, g):
    r = lax.rsqrt(jnp.mean(x * x, axis=-1, keepdims=True) + EPS)
    return (x * r) * g


def _sigmoid(x):
    return 1.0 / (1.0 + jnp.exp(-x))


def _log2(n):
    assert n & (n - 1) == 0
    return n.bit_length() - 1


def _dot(a, b):
    return jnp.dot(a, b, preferred_element_type=F32)


def _dot_nt(a, b):
    return lax.dot_general(a, b, (((1,), (1,)), ((), ())), preferred_element_type=F32)


def _mm(a, b, split, nt=False):
    dot = _dot_nt if nt else _dot
    if not split:
        return dot(a.astype(BF16), b.astype(BF16))
    a, b = a.astype(F32), b.astype(F32)
    a_hi, b_hi = a.astype(BF16), b.astype(BF16)
    a_lo = (a - a_hi.astype(F32)).astype(BF16)
    b_lo = (b - b_hi.astype(F32)).astype(BF16)
    return dot(a_hi, b_hi) + (dot(a_hi, b_lo) + dot(a_lo, b_hi))


def _norm_matmul_kernel(x_ref, g_ref, w_ref, o_ref, h_ref, *, split):
    @pl.when(pl.program_id(1) == 0)
    def _():
        h_ref[...] = _rms(x_ref[...], g_ref[...]).astype(h_ref.dtype)

    o_ref[...] = _mm(h_ref[...], w_ref[...], split)


def norm_matmul(x, g, w, tm, tn, split=False):
    M, D = x.shape
    N = w.shape[1]
    return pl.pallas_call(
        functools.partial(_norm_matmul_kernel, split=split),
        grid=(M // tm, N // tn),
        in_specs=[
            pl.BlockSpec((tm, D), lambda i, j: (i, 0)),
            pl.BlockSpec((1, D), lambda i, j: (0, 0)),
            pl.BlockSpec((D, tn), lambda i, j: (0, j)),
        ],
        out_specs=pl.BlockSpec((tm, tn), lambda i, j: (i, j)),
        out_shape=jax.ShapeDtypeStruct((M, N), F32),
        scratch_shapes=[pltpu.VMEM((tm, D), F32 if split else BF16)],
        compiler_params=_params("parallel", "arbitrary"),
        name="norm_matmul",
    )(x, g.reshape(1, D), w)


def _matmul_residual_kernel(x_ref, a_ref, w_ref, o_ref, *, split):
    o_ref[...] = x_ref[...] + _mm(a_ref[...], w_ref[...], split)


def matmul_residual(x, a, w, tm, split=False):
    M, N = x.shape
    K = a.shape[1]
    return pl.pallas_call(
        functools.partial(_matmul_residual_kernel, split=split),
        grid=(M // tm,),
        in_specs=[
            pl.BlockSpec((tm, N), lambda i: (i, 0)),
            pl.BlockSpec((tm, K), lambda i: (i, 0)),
            pl.BlockSpec((K, N), lambda i: (0, 0)),
        ],
        out_specs=pl.BlockSpec((tm, N), lambda i: (i, 0)),
        out_shape=jax.ShapeDtypeStruct((M, N), F32),
        compiler_params=_params("parallel"),
        name="matmul_residual",
    )(x, a, w)


def _mixers_kernel(a_ref, u_ref, v_ref, ga_ref, gb_ref, pool_prev_ref, conv_prev_ref,
                   wmap_ref, pscale_ref, gv_ref, ws_ref, bs_ref, wdw_ref, bdw_ref, lng_ref, lnb_ref,
                   ya_ref, yb_ref, yd_ref, vn_ref, pool_state_ref, conv_state_ref,
                   aext_ref, cext_ref, *, tt, start_pos, chunk, split):
    t = pl.program_id(1)
    nt = pl.num_programs(1)

    @pl.when(t == 0)
    def _():
        aext_ref[pl.ds(HALO - POOL_STATE, POOL_STATE), :] = pool_prev_ref[0]
        cext_ref[pl.ds(HALO - CONV_STATE, CONV_STATE), :] = conv_prev_ref[0]

    @pl.when(t > 0)
    def _():
        aext_ref[pl.ds(HALO - POOL_STATE, POOL_STATE), :] = aext_ref[pl.ds(HALO + tt - POOL_STATE, POOL_STATE), :]
        cext_ref[pl.ds(HALO - CONV_STATE, CONV_STATE), :] = cext_ref[pl.ds(HALO + tt - CONV_STATE, CONV_STATE), :]

    a_in = a_ref[0]
    aext_ref[pl.ds(HALO, tt), :] = a_in
    c_in = ga_ref[0] * _sigmoid(gb_ref[0])
    cext_ref[pl.ds(HALO, tt), :] = c_in

    pos = start_pos + t * tt + lax.broadcasted_iota(jnp.int32, (tt, 1), 0)
    for gi, w in enumerate(POOL_WINDOWS):
        cols = slice(gi * MIX_GROUP_W, (gi + 1) * MIX_GROUP_W)
        s = a_in[:, cols]
        for j in range(1, w):
            s = s + aext_ref[pl.ds(HALO - j, tt), cols]
        cnt = jnp.minimum(pos + 1, w).astype(F32)
        d = s / cnt - a_in[:, cols]
        y = _mm(d, wmap_ref[gi], split)
        ya_ref[0, :, cols] = (y * pscale_ref[:, cols]).astype(ya_ref.dtype)

    vn = _rms(v_ref[0], gv_ref[...])
    vn_ref[0] = vn
    u = u_ref[0]
    row = lax.broadcasted_iota(jnp.int32, (chunk, chunk), 0)
    col = lax.broadcasted_iota(jnp.int32, (chunk, chunk), 1)
    for gi in range(N_MIX_GROUPS):
        cols = slice(gi * MIX_GROUP_W, (gi + 1) * MIX_GROUP_W)
        ws = jnp.where(row >= col, ws_ref[gi][:chunk, :chunk], 0.0)
        b_col = bs_ref[:, gi:gi + 1]
        for c in range(tt // chunk):
            rows = slice(c * chunk, (c + 1) * chunk)
            vg = vn[rows, cols]
            if chunk == SGU_CHUNK:
                sg = _mm(ws, vg, split)
            else:
                sg = jnp.zeros((chunk, MIX_GROUP_W), F32)
                for k in range(chunk):
                    sg = sg + ws[:, k:k + 1] * vg[k:k + 1, :]
            yb_ref[0, rows, cols] = (u[rows, cols] * (sg + b_col)).astype(yb_ref.dtype)

    rb = min(tt, CONV_ROWS)
    for r0 in range(0, tt, rb):
        acc = jnp.zeros((rb, W_MIX), F32) + bdw_ref[...]
        for j in range(CONV_WIDTH):
            acc = acc + wdw_ref[j:j + 1, :] * cext_ref[pl.ds(HALO - CONV_STATE + r0 + j, rb), :]
        mu = jnp.mean(acc, axis=-1, keepdims=True)
        yc = acc - mu
        var = jnp.mean(yc * yc, axis=-1, keepdims=True)
        yn = (yc * lax.rsqrt(var + EPS)) * lng_ref[...] + lnb_ref[...]
        yd_ref[0, pl.ds(r0, rb), :] = (yn * _sigmoid(yn)).astype(yd_ref.dtype)

    @pl.when(t == nt - 1)
    def _():
        pool_state_ref[0] = aext_ref[pl.ds(HALO + tt - POOL_STATE, POOL_STATE), :]
        conv_state_ref[0] = cext_ref[pl.ds(HALO + tt - CONV_STATE, CONV_STATE), :]


def mixers(proj, pool_prev, conv_prev, wmap, pscale, gv, ws, bs, wdw, bdw, lng, lnb, *, tt, start_pos, act_dtype,
           split=False):
    B, T, _ = proj.shape
    chunk = min(T, SGU_CHUNK)
    bs_t = bs[:, :chunk].T
    col_block = lambda c: pl.BlockSpec((1, tt, W_MIX), lambda b, t: (b, t, c))
    whole = lambda a: pl.BlockSpec(a.shape, lambda b, t: (0,) * a.ndim)
    row2d = lambda a: a.reshape(1, -1)
    pscale, gv, bdw, lng, lnb = map(row2d, (pscale, gv, bdw, lng, lnb))
    act = jax.ShapeDtypeStruct((B, T, W_MIX), act_dtype)
    tile = pl.BlockSpec((1, tt, W_MIX), lambda b, t: (b, t, 0))
    return pl.pallas_call(
        functools.partial(_mixers_kernel, tt=tt, start_pos=start_pos, chunk=chunk, split=split),
        grid=(B, T // tt),
        in_specs=[col_block(0), col_block(1), col_block(2), col_block(6), col_block(7),
                  pl.BlockSpec((1, POOL_STATE, W_MIX), lambda b, t: (b, 0, 0)),
                  pl.BlockSpec((1, CONV_STATE, W_MIX), lambda b, t: (b, 0, 0)),
                  whole(wmap), whole(pscale), whole(gv), whole(ws), whole(bs_t), whole(wdw), whole(bdw),
                  whole(lng), whole(lnb)],
        out_specs=[tile, tile, tile, tile,
                   pl.BlockSpec((1, POOL_STATE, W_MIX), lambda b, t: (b, 0, 0)),
                   pl.BlockSpec((1, CONV_STATE, W_MIX), lambda b, t: (b, 0, 0))],
        out_shape=[act, act, act, jax.ShapeDtypeStruct((B, T, W_MIX), F32),
                   jax.ShapeDtypeStruct((B, POOL_STATE, W_MIX), F32),
                   jax.ShapeDtypeStruct((B, CONV_STATE, W_MIX), F32)],
        scratch_shapes=[pltpu.VMEM((HALO + tt, W_MIX), F32), pltpu.VMEM((HALO + tt, W_MIX), F32)],
        compiler_params=_params("parallel", "arbitrary"),
        name="mixers",
    )(proj, proj, proj, proj, proj, pool_prev, conv_prev, wmap, pscale, gv, ws, bs_t, wdw, bdw, lng, lnb)


def _suffix_matrix():
    row = lax.broadcasted_iota(jnp.int32, (SB_BLOCK, 2 * SB_BLOCK), 0)
    col = lax.broadcasted_iota(jnp.int32, (SB_BLOCK, 2 * SB_BLOCK), 1)
    return jnp.where((row > col) | (col >= SB_BLOCK), 1.0, 0.0).astype(BF16)


def _stick_terms(z, suffix, mask):
    l1p = jnp.log(1.0 + jnp.exp(-jnp.abs(z)))
    log_beta = jnp.minimum(z, 0.0) - l1p
    log_keep = -jnp.maximum(z, 0.0) - l1p
    if mask is not None:
        log_keep = jnp.where(mask, log_keep, 0.0)
    hi = log_keep.astype(BF16)
    lo = (log_keep - hi.astype(F32)).astype(BF16)
    sums = _dot(hi, suffix) + _dot(lo, suffix)
    return log_beta, sums[:, :SB_BLOCK], sums[:, SB_BLOCK:]


def _stick_weights(log_beta, later, mask):
    a = jnp.exp(log_beta + later)
    if mask is not None:
        a = jnp.where(mask, a, 0.0)
    return a.astype(BF16)


def _sb_prompt_kernel(bias_ref, q_ref, k_ref, v_ref, o_ref, qm_ref, carry_ref, acc_ref):
    qi = pl.program_id(1)
    suffix = _suffix_matrix()
    col = lax.broadcasted_iota(jnp.int32, (SB_BLOCK, SB_BLOCK), 1)
    row_all = lax.broadcasted_iota(jnp.int32, carry_ref.shape, 0)
    col_all = lax.broadcasted_iota(jnp.int32, carry_ref.shape, 1)
    causal = (row_all & (SB_BLOCK - 1)) > col_all
    pair_cols = [slice(p * LANES, (p + 1) * LANES) for p in range(SB_HEADS // 2)]
    head_rows = [slice(h * SB_BLOCK, (h + 1) * SB_BLOCK) for h in range(SB_HEADS)]
    for h in range(SB_HEADS):
        own_lanes = (col >= SB_HEAD_DIM) if h % 2 else (col < SB_HEAD_DIM)
        qm_ref[h] = jnp.where(own_lanes, q_ref[0, :, pair_cols[h // 2]] * SB_SCALE, 0.0).astype(BF16)
    carry_ref[...] = jnp.zeros_like(carry_ref)
    acc_ref[...] = jnp.zeros_like(acc_ref)

    def block(kb, mask):
        start = pl.multiple_of(kb * SB_BLOCK, SB_BLOCK)
        zs = []
        for p, cols in enumerate(pair_cols):
            k = k_ref[0, pl.ds(start, SB_BLOCK), cols].astype(BF16)
            zs += [_dot_nt(qm_ref[h], k) + bias_ref[h] for h in (2 * p, 2 * p + 1)]
        log_beta, within, total = _stick_terms(jnp.concatenate(zs, axis=0), suffix, mask)
        carry = carry_ref[...]
        a = _stick_weights(log_beta, within + carry, mask)
        carry_ref[...] = carry + total
        for p, cols in enumerate(pair_cols):
            v = v_ref[0, pl.ds(start, SB_BLOCK), cols].astype(BF16)
            for h in (2 * p, 2 * p + 1):
                acc_ref[h] += _dot(a[head_rows[h]], v)

    block(qi, causal)

    def body(it, _):
        block(qi - 1 - it, None)
        return 0

    lax.fori_loop(0, qi, body, 0)
    for p, cols in enumerate(pair_cols):
        o_ref[0, :, cols] = jnp.where(col < SB_HEAD_DIM, acc_ref[2 * p], acc_ref[2 * p + 1]).astype(o_ref.dtype)


def sb_prompt(proj, bias, act_dtype):
    B, T, _ = proj.shape
    assert SB_BLOCK == LANES and 2 * SB_HEAD_DIM == LANES
    return pl.pallas_call(
        _sb_prompt_kernel,
        grid=(B, T // SB_BLOCK),
        in_specs=[pl.BlockSpec(memory_space=pltpu.SMEM),
                  pl.BlockSpec((1, SB_BLOCK, W_MIX), lambda b, i: (b, i, 3)),
                  pl.BlockSpec((1, T, W_MIX), lambda b, i: (b, 0, 4)),
                  pl.BlockSpec((1, T, W_MIX), lambda b, i: (b, 0, 5))],
        out_specs=pl.BlockSpec((1, SB_BLOCK, W_MIX), lambda b, i: (b, i, 0)),
        out_shape=jax.ShapeDtypeStruct((B, T, W_MIX), act_dtype),
        scratch_shapes=[pltpu.VMEM((SB_HEADS, SB_BLOCK, LANES), BF16),
                        pltpu.VMEM((SB_HEADS * SB_BLOCK, SB_BLOCK), F32),
                        pltpu.VMEM((SB_HEADS, SB_BLOCK, LANES), F32)],
        compiler_params=_params("parallel", "arbitrary"),
        name="sb_prompt",
    )(bias, proj, proj, proj)


def _sb_sample_kernel(pt_ref, q_ref, bias_ref, knew_ref, vnew_ref, *refs, pages_per_step):
    k_refs = refs[:pages_per_step]
    v_refs = refs[pages_per_step:2 * pages_per_step]
    o_ref, carry_ref, acc_ref = refs[2 * pages_per_step:]
    s = pl.program_id(1)
    suffix = _suffix_matrix()
    q = q_ref[0]
    bias = bias_ref[...]
    rows = q.shape[0]
    flat = lambda page_ref: page_ref[0, 0].reshape(SB_HEADS * SB_HEAD_DIM, SB_BLOCK).astype(BF16)

    def run(pages, mask, carry, acc):
        z = jnp.concatenate([_dot(q, flat(kt_ref)) + bias for kt_ref, _ in pages], axis=0)
        log_beta, within, total = _stick_terms(z, suffix, mask)
        later = []
        for j in range(len(pages)):
            page_rows = slice(j * rows, (j + 1) * rows)
            later.append(within[page_rows] + carry)
            carry = carry + total[page_rows]
        a = _stick_weights(log_beta, jnp.concatenate(later, axis=0), mask)
        for j, (_, vt_ref) in enumerate(pages):
            acc = acc + _dot_nt(a[j * rows:(j + 1) * rows], flat(vt_ref))
        carry_ref[...] = carry
        acc_ref[...] = acc

    @pl.when(s == 0)
    def _():
        qidx = lax.broadcasted_iota(jnp.int32, carry_ref.shape, 0) & (SB_QROWS - 1)
        kidx = lax.broadcasted_iota(jnp.int32, carry_ref.shape, 1)
        run([(knew_ref, vnew_ref)], qidx > kidx, jnp.zeros(carry_ref.shape, F32), jnp.zeros(acc_ref.shape, F32))

    run(list(zip(k_refs, v_refs)), None, carry_ref[...], acc_ref[...])

    @pl.when(s == pl.num_programs(1) - 1)
    def _():
        row_head = lax.broadcasted_iota(jnp.int32, acc_ref.shape, 0) >> _log2(SB_QROWS)
        col_head = lax.broadcasted_iota(jnp.int32, acc_ref.shape, 1) >> _log2(SB_HEAD_DIM)
        own = jnp.where(row_head == col_head, acc_ref[...], 0.0)
        o_ref[0] = sum(own[:, h * SB_HEAD_DIM:(h + 1) * SB_HEAD_DIM] for h in range(SB_HEADS))


def sb_sample(q, k_new, v_new, bias, cache_k, cache_v, layer, page_table):
    DB, t_new, _ = q.shape
    assert t_new <= SB_QROWS
    n_pages = page_table.shape[1]
    pages_per_step = math.gcd(SB_PAGES_PER_STEP, n_pages)
    rows = SB_HEADS * SB_QROWS
    heads = lambda a: a.reshape(DB, t_new, SB_HEADS, SB_HEAD_DIM)
    q_hm = jnp.pad(heads(q * SB_SCALE).transpose(0, 2, 1, 3), ((0, 0), (0, 0), (0, SB_QROWS - t_new), (0, 0)))
    own = jnp.eye(SB_HEADS, dtype=F32)[None, :, None, :, None]
    q_blk = (q_hm[:, :, :, None, :] * own).reshape(DB, rows, W_MIX).astype(BF16)
    bias_col = jnp.broadcast_to(jnp.repeat(bias, SB_QROWS)[:, None], (rows, SB_BLOCK)).astype(F32)
    as_page = lambda a: jnp.pad(heads(a).transpose(0, 2, 3, 1), ((0, 0), (0, 0), (0, 0), (0, SB_BLOCK - t_new)))[:, None]

    def page_spec(j):
        def index(b, s, pt):
            return (layer, pt[b, n_pages - 1 - (s * pages_per_step + j)], 0, 0, 0)
        return pl.BlockSpec((1, 1, SB_HEADS, SB_HEAD_DIM, SB_BLOCK), index)

    per_seq = lambda shape: pl.BlockSpec((1,) + shape, lambda b, s, pt: (b,) + (0,) * len(shape))
    new_page = per_seq((1, SB_HEADS, SB_HEAD_DIM, SB_BLOCK))
    grid_spec = pltpu.PrefetchScalarGridSpec(
        num_scalar_prefetch=1,
        grid=(DB, n_pages // pages_per_step),
        in_specs=[per_seq((rows, W_MIX)), pl.BlockSpec((rows, SB_BLOCK), lambda b, s, pt: (0, 0)), new_page, new_page]
        + [page_spec(j) for j in range(pages_per_step)] * 2,
        out_specs=per_seq((rows, SB_HEAD_DIM)),
        scratch_shapes=[pltpu.VMEM((rows, SB_BLOCK), F32), pltpu.VMEM((rows, W_MIX), F32)],
    )
    o = pl.pallas_call(
        functools.partial(_sb_sample_kernel, pages_per_step=pages_per_step),
        grid_spec=grid_spec,
        out_shape=jax.ShapeDtypeStruct((DB, rows, SB_HEAD_DIM), F32),
        compiler_params=_params("parallel", "arbitrary"),
        name="sb_sample",
    )(page_table, q_blk, bias_col, as_page(k_new), as_page(v_new), *([cache_k] * pages_per_step),
      *([cache_v] * pages_per_step))
    o = o.reshape(DB, SB_HEADS, SB_QROWS, SB_HEAD_DIM)[:, :, :t_new]
    return o.transpose(0, 2, 1, 3).reshape(DB, t_new, W_MIX)


def _merge_kernel(x_ref, ya_ref, yb_ref, yc_ref, yd_ref, gate_ref, bg_ref, wbr_ref, wout_ref, o_ref, *, split):
    D = x_ref.shape[-1]
    merged = jnp.zeros(x_ref.shape, F32)
    for i, y_ref in enumerate((ya_ref, yb_ref, yc_ref, yd_ref)):
        gate = _sigmoid(gate_ref[:, i * D:(i + 1) * D] + bg_ref[i:i + 1, :])
        merged = merged + gate * _mm(y_ref[...], wbr_ref[i], split)
    o_ref[...] = x_ref[...] + _mm(merged, wout_ref[...], split)


def merge_out(x, ya, yb, yc, yd, proj, b_gate, w_br, w_out, tm, split=False):
    M, D = x.shape
    act = pl.BlockSpec((tm, W_MIX), lambda i: (i, 0))
    whole = lambda a: pl.BlockSpec(a.shape, lambda i: (0,) * a.ndim)
    return pl.pallas_call(
        functools.partial(_merge_kernel, split=split),
        grid=(M // tm,),
        in_specs=[pl.BlockSpec((tm, D), lambda i: (i, 0)), act, act, act, act,
                  pl.BlockSpec((tm, 4 * D), lambda i: (i, 1)), whole(b_gate), whole(w_br), whole(w_out)],
        out_specs=pl.BlockSpec((tm, D), lambda i: (i, 0)),
        out_shape=jax.ShapeDtypeStruct((M, D), F32),
        compiler_params=_params("parallel"),
        name="merge_out",
    )(x, ya, yb, yc, yd, proj, b_gate, w_br, w_out)


def _mem_attn_kernel(q_ref, k_ref, v_ref, o_ref, *, split):
    dh = q_ref.shape[-1] // MEM_HEADS
    scale = dh ** -0.5
    for h in range(MEM_HEADS):
        cols = slice(h * dh, (h + 1) * dh)
        s = _mm(q_ref[0, :, cols], k_ref[0, :, cols], split, nt=True) * scale
        e = jnp.exp(s - jnp.max(s, axis=-1, keepdims=True))
        p = e / jnp.sum(e, axis=-1, keepdims=True)
        o_ref[0, :, cols] = _mm(p, v_ref[0, :, cols], split).astype(o_ref.dtype)


def mem_attn(q, mem_k, mem_v, tt, act_dtype, split=False):
    B, T, D = q.shape
    n_mem = mem_k.shape[1]
    mem = pl.BlockSpec((1, n_mem, D), lambda b, t: (b, 0, 0))
    return pl.pallas_call(
        functools.partial(_mem_attn_kernel, split=split),
        grid=(B, T // tt),
        in_specs=[pl.BlockSpec((1, tt, D), lambda b, t: (b, t, 0)), mem, mem],
        out_specs=pl.BlockSpec((1, tt, D), lambda b, t: (b, t, 0)),
        out_shape=jax.ShapeDtypeStruct((B, T, D), act_dtype),
        compiler_params=_params("parallel", "arbitrary"),
        name="mem_attn",
    )(q, mem_k, mem_v)


def _route(logits):
    lane_i = lax.broadcasted_iota(jnp.int32, logits.shape, 1)
    lane = lane_i.astype(F32)
    neg = -jnp.inf
    big = float(LANES)
    lg = jnp.where(lane_i < N_GROUPS, logits, neg)
    g_max = jnp.max(lg, axis=-1, keepdims=True)
    g_w = 1.0 / jnp.sum(jnp.exp(lg - g_max), axis=-1, keepdims=True)
    g_idx = jnp.min(jnp.where(lg == g_max, lane, big), axis=-1, keepdims=True)
    e_lane = lane_i - N_GROUPS
    e_group = (e_lane >> _log2(EXPERTS_PER_GROUP)).astype(F32)
    in_group = (e_lane >= 0) & (e_lane < N_EXPERTS) & (e_group == g_idx)
    le = jnp.where(in_group, logits, neg)
    v1 = jnp.max(le, axis=-1, keepdims=True)
    i1 = jnp.min(jnp.where(in_group & (le == v1), lane, big), axis=-1, keepdims=True)
    rest = in_group & (lane != i1)
    le2 = jnp.where(rest, logits, neg)
    v2 = jnp.max(le2, axis=-1, keepdims=True)
    i2 = jnp.min(jnp.where(rest & (le2 == v2), lane, big), axis=-1, keepdims=True)
    e2 = jnp.exp(v2 - v1)
    w1 = 1.0 / (1.0 + e2)
    w2 = e2 / (1.0 + e2)
    return g_w * (jnp.where(lane == i1, w1, 0.0) + jnp.where(lane == i2, w2, 0.0))


def _moe_kernel(x_ref, g_ref, wr_ref, br_ref, w13_ref, w2_ref, gf_ref, o_ref, h_ref, comb_ref, acc_ref, *, final_norm,
                split):
    e = pl.program_id(1)

    @pl.when(e == 0)
    def _():
        h = _rms(x_ref[...], g_ref[...])
        h_ref[...] = h.astype(h_ref.dtype)
        comb_ref[...] = _route(_mm(h, wr_ref[...], split) + br_ref[...])
        acc_ref[...] = jnp.zeros_like(acc_ref)

    de = w2_ref.shape[1]
    lane = lax.broadcasted_iota(jnp.int32, comb_ref.shape, 1)
    weight = jnp.sum(jnp.where(lane == e + N_GROUPS, comb_ref[...], 0.0), axis=-1, keepdims=True)
    hid13 = _mm(h_ref[...], w13_ref[0], split)
    g1 = hid13[:, :de]
    hid = (g1 * _sigmoid(g1)) * hid13[:, de:]
    acc_ref[...] += weight * _mm(hid, w2_ref[0], split)

    @pl.when(e == pl.num_programs(1) - 1)
    def _():
        y = x_ref[...] + acc_ref[...]
        o_ref[...] = _rms(y, gf_ref[...]) if final_norm else y


def moe(x, g, w_router, b_router, w13, w2, g_final, tm, final_norm, split=False):
    M, D = x.shape
    de = w2.shape[1]
    vec = lambda a: pl.BlockSpec(a.shape, lambda i, e: (0,) * a.ndim)
    g, g_final = g.reshape(1, D), g_final.reshape(1, D)
    return pl.pallas_call(
        functools.partial(_moe_kernel, final_norm=final_norm, split=split),
        grid=(M // tm, N_EXPERTS),
        in_specs=[pl.BlockSpec((tm, D), lambda i, e: (i, 0)), vec(g), vec(w_router), vec(b_router),
                  pl.BlockSpec((1, D, 2 * de), lambda i, e: (e, 0, 0)),
                  pl.BlockSpec((1, de, D), lambda i, e: (e, 0, 0)), vec(g_final)],
        out_specs=pl.BlockSpec((tm, D), lambda i, e: (i, 0)),
        out_shape=jax.ShapeDtypeStruct((M, D), F32),
        scratch_shapes=[pltpu.VMEM((tm, D), F32 if split else BF16), pltpu.VMEM((tm, LANES), F32),
                        pltpu.VMEM((tm, D), F32)],
        compiler_params=_params("parallel", "arbitrary"),
        name="moe",
    )(x, g, w_router, b_router, w13, w2, g_final)


def _layer(x, lw, pool_prev, conv_prev, mem_k, mem_v, sb_fn, *, start_pos, tiles, act_dtype, final_norm, split):
    B, T, D = x.shape
    M = B * T
    x2 = x.reshape(M, D)
    proj = norm_matmul(x2, lw["g_mix"], lw["w_in"], tiles["tm_in"], tiles["tn_in"], split)
    proj3 = proj.reshape(B, T, -1)
    ya, yb, yd, vn, pool_state, conv_state = mixers(
        proj3, pool_prev, conv_prev, lw["w_pool_map"], lw["pool_scale"], lw["g_sgu_v"], lw["w_sgu_s"], lw["b_sgu"],
        lw["w_dw"], lw["b_dw"], lw["ln_g"], lw["ln_b"], tt=tiles["tt_mix"], start_pos=start_pos, act_dtype=act_dtype,
        split=split)
    k_new = proj3[:, :, 4 * W_MIX:5 * W_MIX]
    v_new = proj3[:, :, 5 * W_MIX:6 * W_MIX]
    yc = sb_fn(proj3, k_new, v_new)
    flat = lambda a: a.reshape(M, W_MIX)
    x2 = merge_out(x2, flat(ya), flat(yb), flat(yc), flat(yd), proj, lw["b_gate"], lw["w_br"], lw["w_out"],
                   tiles["tm_merge"], split)
    q = norm_matmul(x2, lw["g_xattn"], lw["w_xq"], tiles["tm_in"], tiles["tn_in"], split)
    o = mem_attn(q.reshape(B, T, D), mem_k, mem_v, tiles["tt_mem"], act_dtype, split)
    x2 = matmul_residual(x2, o.reshape(M, D), lw["w_xo"], tiles["tm_merge"], split)
    x2 = moe(x2, lw["g_moe"], lw["w_router"], lw["b_router"], lw["w_e13"], lw["w_e2"], lw["g_final"], tiles["tm_moe"],
             final_norm, split)
    heads = lambda a: a.reshape(B, T, SB_HEADS, SB_HEAD_DIM)
    return x2.reshape(B, T, D), pool_state, conv_state, heads(k_new), heads(v_new), vn


def kernel(x_prompt, x_sample, mem_prompt, cache_k, cache_v, cache_mem_k, cache_mem_v, state_pool, state_conv,
           page_table, g_mix, w_in, b_gate, w_pool_map, pool_scale, w_br_pool, g_sgu_v, w_sgu_s, b_sgu, w_br_sgu,
           sb_bias, w_br_sb, w_dw, b_dw, ln_g, ln_b, w_br_conv, w_out, g_xattn, g_mem, w_xq, w_xk, w_xv, w_xo,
           g_moe, w_rg, b_rg, w_re, b_re, w_e1, w_e3, w_e2, g_final):
    B, T, D = x_prompt.shape
    DB, TS, _ = x_sample.shape
    depth = w_in.shape[0]
    n_mem = mem_prompt.shape[1]
    past_len = page_table.shape[1] * cache_k.shape[2]
    cache_kt = cache_k.transpose(0, 1, 3, 4, 2)
    cache_vt = cache_v.transpose(0, 1, 3, 4, 2)
    pad = LANES - N_GROUPS - N_EXPERTS

    prompt_tiles = dict(tm_in=min(B * T, 1024), tn_in=1024, tt_mix=min(T, 256), tm_merge=min(B * T, 256),
                        tt_mem=min(T, 512), tm_moe=min(B * T, 1024))
    sample_tiles = dict(tm_in=DB * TS, tn_in=1024, tt_mix=TS, tm_merge=DB * TS, tt_mem=TS, tm_moe=DB * TS)

    xp, xs = x_prompt, x_sample
    outs = [[] for _ in range(11)]
    for l in range(depth):
        lw32 = dict(
            g_mix=g_mix[l], w_in=w_in[l], b_gate=b_gate[l], w_pool_map=w_pool_map[l],
            pool_scale=pool_scale[l], g_sgu_v=g_sgu_v[l], w_sgu_s=w_sgu_s[l], b_sgu=b_sgu[l],
            w_dw=w_dw[l], b_dw=b_dw[l], ln_g=ln_g[l], ln_b=ln_b[l],
            w_br=jnp.stack([w_br_pool[l], w_br_sgu[l], w_br_sb[l], w_br_conv[l]]),
            w_out=w_out[l], g_xattn=g_xattn[l], w_xq=w_xq[l], w_xo=w_xo[l], g_moe=g_moe[l],
            w_router=jnp.pad(jnp.concatenate([w_rg[l], w_re[l]], axis=1), ((0, 0), (0, pad))),
            b_router=jnp.pad(jnp.concatenate([b_rg[l], b_re[l]]), (0, pad)).reshape(1, LANES),
            w_e13=jnp.concatenate([w_e1[l], w_e3[l]], axis=-1), w_e2=w_e2[l], g_final=g_final)
        mxu_weights = ("w_in", "w_pool_map", "w_br", "w_out", "w_xq", "w_xo", "w_router", "w_e13", "w_e2")
        lw = {k: v.astype(BF16) if k in mxu_weights else v for k, v in lw32.items()}
        final_norm = l == depth - 1

        w_kv = jnp.concatenate([w_xk[l], w_xv[l]], axis=1).astype(BF16)
        mkv = norm_matmul(mem_prompt.reshape(B * n_mem, D), g_mem[l], w_kv, min(B * n_mem, 1024), 1024)
        mk = mkv[:, :D].reshape(B, n_mem, D)
        mv = mkv[:, D:].reshape(B, n_mem, D)
        sb_p = lambda proj3, k_new, v_new: sb_prompt(proj3, sb_bias[l], BF16)
        xp, pool_p, conv_p, k_p, v_p, _ = _layer(
            xp, lw, jnp.zeros((B, POOL_STATE, W_MIX), F32), jnp.zeros((B, CONV_STATE, W_MIX), F32), mk, mv, sb_p,
            start_pos=0, tiles=prompt_tiles, act_dtype=BF16, final_norm=final_norm, split=False)

        sb_s = lambda proj3, k_new, v_new: sb_sample(
            proj3[:, :, 3 * W_MIX:4 * W_MIX], k_new, v_new, sb_bias[l], cache_kt, cache_vt, l, page_table)
        xs, pool_s, conv_s, k_s, v_s, vn_s = _layer(
            xs, lw32, state_pool[l], state_conv[l], cache_mem_k[l].reshape(DB, n_mem, D),
            cache_mem_v[l].reshape(DB, n_mem, D), sb_s, start_pos=past_len, tiles=sample_tiles, act_dtype=F32,
            final_norm=final_norm, split=True)

        heads = lambda a: a.reshape(B, n_mem, MEM_HEADS, D // MEM_HEADS)
        for lst, val in zip(outs, (k_p, v_p, k_s, v_s, pool_p, pool_s, conv_p, conv_s, vn_s, heads(mk), heads(mv))):
            lst.append(val)
    return (xp, xs) + tuple(jnp.stack(o) for o in outs)
```

```python
import functools
import math

import jax
import jax.numpy as jnp
from jax import lax
from jax.experimental import pallas as pl
from jax.experimental.pallas import tpu as pltpu

F32 = jnp.float32
BF16 = jnp.bfloat16
EPS = 1e-6

LANES = 128
W_MIX = 512
POOL_WINDOWS = (2, 4, 8, 16)
POOL_STATE = max(POOL_WINDOWS) - 1
N_MIX_GROUPS = 4
MIX_GROUP_W = W_MIX // N_MIX_GROUPS
SGU_CHUNK = 128
SB_HEADS = 8
SB_HEAD_DIM = 64
SB_SCALE = SB_HEAD_DIM ** -0.5
SB_BLOCK = 128
SB_PAGES_PER_STEP = 16
SB_QROWS = 8
CONV_WIDTH = 31
CONV_STATE = CONV_WIDTH - 1
HALO = 32
CONV_ROWS = 32
MEM_HEADS = 4
N_GROUPS = 4
EXPERTS_PER_GROUP = 8
N_EXPERTS = N_GROUPS * EXPERTS_PER_GROUP
VMEM_LIMIT = 56 * 1024 * 1024


def _params(*sem):
    return pltpu.CompilerParams(dimension_semantics=sem, vmem_limit_bytes=VMEM_LIMIT)


def _rms(x, g):
    r = lax.rsqrt(jnp.mean(x * x, axis=-1, keepdims=True) + EPS)
    return (x * r) * g


def _sigmoid(x):
    return 1.0 / (1.0 + jnp.exp(-x))


def _log2(n):
    assert n & (n - 1) == 0
    return n.bit_length() - 1


def _dot(a, b):
    return jnp.dot(a, b, preferred_element_type=F32)


def _dot_nt(a, b):
    return lax.dot_general(a, b, (((1,), (1,)), ((), ())), preferred_element_type=F32)


def _mm(a, b, split, nt=False):
    dot = _dot_nt if nt else _dot
    if not split:
        return dot(a.astype(BF16), b.astype(BF16))
    a, b = a.astype(F32), b.astype(F32)
    a_hi, b_hi = a.astype(BF16), b.astype(BF16)
    a_lo = (a - a_hi.astype(F32)).astype(BF16)
    b_lo = (b - b_hi.astype(F32)).astype(BF16)
    return dot(a_hi, b_hi) + (dot(a_hi, b_lo) + dot(a_lo, b_hi))


def _norm_matmul_kernel(x_ref, g_ref, w_ref, o_ref, h_ref, *, split):
    @pl.when(pl.program_id(1) == 0)
    def _():
        h_ref[...] = _rms(x_ref[...], g_ref[...]).astype(h_ref.dtype)

    o_ref[...] = _mm(h_ref[...], w_ref[...], split)


def norm_matmul(x, g, w, tm, tn, split=False):
    M, D = x.shape
    N = w.shape[1]
    return pl.pallas_call(
        functools.partial(_norm_matmul_kernel, split=split),
        grid=(M // tm, N // tn),
        in_specs=[
            pl.BlockSpec((tm, D), lambda i, j: (i, 0)),
            pl.BlockSpec((1, D), lambda i, j: (0, 0)),
            pl.BlockSpec((D, tn), lambda i, j: (0, j)),
        ],
        out_specs=pl.BlockSpec((tm, tn), lambda i, j: (i, j)),
        out_shape=jax.ShapeDtypeStruct((M, N), F32),
        scratch_shapes=[pltpu.VMEM((tm, D), F32 if split else BF16)],
        compiler_params=_params("parallel", "arbitrary"),
        name="norm_matmul",
    )(x, g.reshape(1, D), w)


def _matmul_residual_kernel(x_ref, a_ref, w_ref, o_ref, *, split):
    o_ref[...] = x_ref[...] + _mm(a_ref[...], w_ref[...], split)


def matmul_residual(x, a, w, tm, split=False):
    M, N = x.shape
    K = a.shape[1]
    return pl.pallas_call(
        functools.partial(_matmul_residual_kernel, split=split),
        grid=(M // tm,),
        in_specs=[
            pl.BlockSpec((tm, N), lambda i: (i, 0)),
            pl.BlockSpec((tm, K), lambda i: (i, 0)),
            pl.BlockSpec((K, N), lambda i: (0, 0)),
        ],
        out_specs=pl.BlockSpec((tm, N), lambda i: (i, 0)),
        out_shape=jax.ShapeDtypeStruct((M, N), F32),
        compiler_params=_params("parallel"),
        name="matmul_residual",
    )(x, a, w)


def _mixers_kernel(a_ref, u_ref, v_ref, ga_ref, gb_ref, pool_prev_ref, conv_prev_ref,
                   wmap_ref, pscale_ref, gv_ref, ws_ref, bs_ref, wdw_ref, bdw_ref, lng_ref, lnb_ref,
                   ya_ref, yb_ref, yd_ref, vn_ref, pool_state_ref, conv_state_ref,
                   aext_ref, cext_ref, *, tt, start_pos, chunk, split):
    t = pl.program_id(1)
    nt = pl.num_programs(1)

    @pl.when(t == 0)
    def _():
        aext_ref[pl.ds(HALO - POOL_STATE, POOL_STATE), :] = pool_prev_ref[0]
        cext_ref[pl.ds(HALO - CONV_STATE, CONV_STATE), :] = conv_prev_ref[0]

    @pl.when(t > 0)
    def _():
        aext_ref[pl.ds(HALO - POOL_STATE, POOL_STATE), :] = aext_ref[pl.ds(HALO + tt - POOL_STATE, POOL_STATE), :]
        cext_ref[pl.ds(HALO - CONV_STATE, CONV_STATE), :] = cext_ref[pl.ds(HALO + tt - CONV_STATE, CONV_STATE), :]

    a_in = a_ref[0]
    aext_ref[pl.ds(HALO, tt), :] = a_in
    c_in = ga_ref[0] * _sigmoid(gb_ref[0])
    cext_ref[pl.ds(HALO, tt), :] = c_in

    pos = start_pos + t * tt + lax.broadcasted_iota(jnp.int32, (tt, 1), 0)
    for gi, w in enumerate(POOL_WINDOWS):
        cols = slice(gi * MIX_GROUP_W, (gi + 1) * MIX_GROUP_W)
        s = a_in[:, cols]
        for j in range(1, w):
            s = s + aext_ref[pl.ds(HALO - j, tt), cols]
        cnt = jnp.minimum(pos + 1, w).astype(F32)
        d = s / cnt - a_in[:, cols]
        y = _mm(d, wmap_ref[gi], split)
        ya_ref[0, :, cols] = (y * pscale_ref[:, cols]).astype(ya_ref.dtype)

    vn = _rms(v_ref[0], gv_ref[...])
    vn_ref[0] = vn
    u = u_ref[0]
    row = lax.broadcasted_iota(jnp.int32, (chunk, chunk), 0)
    col = lax.broadcasted_iota(jnp.int32, (chunk, chunk), 1)
    for gi in range(N_MIX_GROUPS):
        cols = slice(gi * MIX_GROUP_W, (gi + 1) * MIX_GROUP_W)
        ws = jnp.where(row >= col, ws_ref[gi][:chunk, :chunk], 0.0)
        b_col = bs_ref[:, gi:gi + 1]
        for c in range(tt // chunk):
            rows = slice(c * chunk, (c + 1) * chunk)
            vg = vn[rows, cols]
            if chunk == SGU_CHUNK:
                sg = _mm(ws, vg, split)
            else:
                sg = jnp.zeros((chunk, MIX_GROUP_W), F32)
                for k in range(chunk):
                    sg = sg + ws[:, k:k + 1] * vg[k:k + 1, :]
            yb_ref[0, rows, cols] = (u[rows, cols] * (sg + b_col)).astype(yb_ref.dtype)

    rb = min(tt, CONV_ROWS)
    for r0 in range(0, tt, rb):
        acc = jnp.zeros((rb, W_MIX), F32) + bdw_ref[...]
        for j in range(CONV_WIDTH):
            acc = acc + wdw_ref[j:j + 1, :] * cext_ref[pl.ds(HALO - CONV_STATE + r0 + j, rb), :]
        mu = jnp.mean(acc, axis=-1, keepdims=True)
        yc = acc - mu
        var = jnp.mean(yc * yc, axis=-1, keepdims=True)
        yn = (yc * lax.rsqrt(var + EPS)) * lng_ref[...] + lnb_ref[...]
        yd_ref[0, pl.ds(r0, rb), :] = (yn * _sigmoid(yn)).astype(yd_ref.dtype)

    @pl.when(t == nt - 1)
    def _():
        pool_state_ref[0] = aext_ref[pl.ds(HALO + tt - POOL_STATE, POOL_STATE), :]
        conv_state_ref[0] = cext_ref[pl.ds(HALO + tt - CONV_STATE, CONV_STATE), :]


def mixers(proj, pool_prev, conv_prev, wmap, pscale, gv, ws, bs, wdw, bdw, lng, lnb, *, tt, start_pos, act_dtype,
           split=False):
    B, T, _ = proj.shape
    chunk = min(T, SGU_CHUNK)
    bs_t = bs[:, :chunk].T
    col_block = lambda c: pl.BlockSpec((1, tt, W_MIX), lambda b, t: (b, t, c))
    whole = lambda a: pl.BlockSpec(a.shape, lambda b, t: (0,) * a.ndim)
    row2d = lambda a: a.reshape(1, -1)
    pscale, gv, bdw, lng, lnb = map(row2d, (pscale, gv, bdw, lng, lnb))
    act = jax.ShapeDtypeStruct((B, T, W_MIX), act_dtype)
    tile = pl.BlockSpec((1, tt, W_MIX), lambda b, t: (b, t, 0))
    return pl.pallas_call(
        functools.partial(_mixers_kernel, tt=tt, start_pos=start_pos, chunk=chunk, split=split),
        grid=(B, T // tt),
        in_specs=[col_block(0), col_block(1), col_block(2), col_block(6), col_block(7),
                  pl.BlockSpec((1, POOL_STATE, W_MIX), lambda b, t: (b, 0, 0)),
                  pl.BlockSpec((1, CONV_STATE, W_MIX), lambda b, t: (b, 0, 0)),
                  whole(wmap), whole(pscale), whole(gv), whole(ws), whole(bs_t), whole(wdw), whole(bdw),
                  whole(lng), whole(lnb)],
        out_specs=[tile, tile, tile, tile,
                   pl.BlockSpec((1, POOL_STATE, W_MIX), lambda b, t: (b, 0, 0)),
                   pl.BlockSpec((1, CONV_STATE, W_MIX), lambda b, t: (b, 0, 0))],
        out_shape=[act, act, act, jax.ShapeDtypeStruct((B, T, W_MIX), F32),
                   jax.ShapeDtypeStruct((B, POOL_STATE, W_MIX), F32),
                   jax.ShapeDtypeStruct((B, CONV_STATE, W_MIX), F32)],
        scratch_shapes=[pltpu.VMEM((HALO + tt, W_MIX), F32), pltpu.VMEM((HALO + tt, W_MIX), F32)],
        compiler_params=_params("parallel", "arbitrary"),
        name="mixers",
    )(proj, proj, proj, proj, proj, pool_prev, conv_prev, wmap, pscale, gv, ws, bs_t, wdw, bdw, lng, lnb)


def _suffix_matrix():
    row = lax.broadcasted_iota(jnp.int32, (SB_BLOCK, 2 * SB_BLOCK), 0)
    col = lax.broadcasted_iota(jnp.int32, (SB_BLOCK, 2 * SB_BLOCK), 1)
    return jnp.where((row > col) | (col >= SB_BLOCK), 1.0, 0.0).astype(BF16)


def _stick_terms(z, suffix, mask):
    l1p = jnp.log(1.0 + jnp.exp(-jnp.abs(z)))
    log_beta = jnp.minimum(z, 0.0) - l1p
    log_keep = -jnp.maximum(z, 0.0) - l1p
    if mask is not None:
        log_keep = jnp.where(mask, log_keep, 0.0)
    hi = log_keep.astype(BF16)
    lo = (log_keep - hi.astype(F32)).astype(BF16)
    sums = _dot(hi, suffix) + _dot(lo, suffix)
    return log_beta, sums[:, :SB_BLOCK], sums[:, SB_BLOCK:]


def _stick_weights(log_beta, later, mask):
    a = jnp.exp(log_beta + later)
    if mask is not None:
        a = jnp.where(mask, a, 0.0)
    return a.astype(BF16)


def _sb_prompt_kernel(bias_ref, q_ref, k_ref, v_ref, o_ref, qm_ref, carry_ref, acc_ref):
    qi = pl.program_id(1)
    suffix = _suffix_matrix()
    col = lax.broadcasted_iota(jnp.int32, (SB_BLOCK, SB_BLOCK), 1)
    row_all = lax.broadcasted_iota(jnp.int32, carry_ref.shape, 0)
    col_all = lax.broadcasted_iota(jnp.int32, carry_ref.shape, 1)
    causal = (row_all & (SB_BLOCK - 1)) > col_all
    pair_cols = [slice(p * LANES, (p + 1) * LANES) for p in range(SB_HEADS // 2)]
    head_rows = [slice(h * SB_BLOCK, (h + 1) * SB_BLOCK) for h in range(SB_HEADS)]
    for h in range(SB_HEADS):
        own_lanes = (col >= SB_HEAD_DIM) if h % 2 else (col < SB_HEAD_DIM)
        qm_ref[h] = jnp.where(own_lanes, q_ref[0, :, pair_cols[h // 2]] * SB_SCALE, 0.0).astype(BF16)
    carry_ref[...] = jnp.zeros_like(carry_ref)
    acc_ref[...] = jnp.zeros_like(acc_ref)

    def block(kb, mask):
        start = pl.multiple_of(kb * SB_BLOCK, SB_BLOCK)
        zs = []
        for p, cols in enumerate(pair_cols):
            k = k_ref[0, pl.ds(start, SB_BLOCK), cols].astype(BF16)
            zs += [_dot_nt(qm_ref[h], k) + bias_ref[h] for h in (2 * p, 2 * p + 1)]
        log_beta, within, total = _stick_terms(jnp.concatenate(zs, axis=0), suffix, mask)
        carry = carry_ref[...]
        a = _stick_weights(log_beta, within + carry, mask)
        carry_ref[...] = carry + total
        for p, cols in enumerate(pair_cols):
            v = v_ref[0, pl.ds(start, SB_BLOCK), cols].astype(BF16)
            for h in (2 * p, 2 * p + 1):
                acc_ref[h] += _dot(a[head_rows[h]], v)

    block(qi, causal)

    def body(it, _):
        block(qi - 1 - it, None)
        return 0

    lax.fori_loop(0, qi, body, 0)
    for p, cols in enumerate(pair_cols):
        o_ref[0, :, cols] = jnp.where(col < SB_HEAD_DIM, acc_ref[2 * p], acc_ref[2 * p + 1]).astype(o_ref.dtype)


def sb_prompt(proj, bias, act_dtype):
    B, T, _ = proj.shape
    assert SB_BLOCK == LANES and 2 * SB_HEAD_DIM == LANES
    return pl.pallas_call(
        _sb_prompt_kernel,
        grid=(B, T // SB_BLOCK),
        in_specs=[pl.BlockSpec(memory_space=pltpu.SMEM),
                  pl.BlockSpec((1, SB_BLOCK, W_MIX), lambda b, i: (b, i, 3)),
                  pl.BlockSpec((1, T, W_MIX), lambda b, i: (b, 0, 4)),
                  pl.BlockSpec((1, T, W_MIX), lambda b, i: (b, 0, 5))],
        out_specs=pl.BlockSpec((1, SB_BLOCK, W_MIX), lambda b, i: (b, i, 0)),
        out_shape=jax.ShapeDtypeStruct((B, T, W_MIX), act_dtype),
        scratch_shapes=[pltpu.VMEM((SB_HEADS, SB_BLOCK, LANES), BF16),
                        pltpu.VMEM((SB_HEADS * SB_BLOCK, SB_BLOCK), F32),
                        pltpu.VMEM((SB_HEADS, SB_BLOCK, LANES), F32)],
        compiler_params=_params("parallel", "arbitrary"),
        name="sb_prompt",
    )(bias, proj, proj, proj)


def _sb_sample_kernel(pt_ref, q_ref, bias_ref, knew_ref, vnew_ref, *refs, pages_per_step):
    k_refs = refs[:pages_per_step]
    v_refs = refs[pages_per_step:2 * pages_per_step]
    o_ref, carry_ref, acc_ref = refs[2 * pages_per_step:]
    s = pl.program_id(1)
    suffix = _suffix_matrix()
    q = q_ref[0]
    bias = bias_ref[...]
    rows = q.shape[0]
    flat = lambda page_ref: page_ref[0, 0].reshape(SB_HEADS * SB_HEAD_DIM, SB_BLOCK).astype(BF16)

    def run(pages, mask, carry, acc):
        z = jnp.concatenate([_dot(q, flat(kt_ref)) + bias for kt_ref, _ in pages], axis=0)
        log_beta, within, total = _stick_terms(z, suffix, mask)
        later = []
        for j in range(len(pages)):
            page_rows = slice(j * rows, (j + 1) * rows)
            later.append(within[page_rows] + carry)
            carry = carry + total[page_rows]
        a = _stick_weights(log_beta, jnp.concatenate(later, axis=0), mask)
        for j, (_, vt_ref) in enumerate(pages):
            acc = acc + _dot_nt(a[j * rows:(j + 1) * rows], flat(vt_ref))
        carry_ref[...] = carry
        acc_ref[...] = acc

    @pl.when(s == 0)
    def _():
        qidx = lax.broadcasted_iota(jnp.int32, carry_ref.shape, 0) & (SB_QROWS - 1)
        kidx = lax.broadcasted_iota(jnp.int32, carry_ref.shape, 1)
        run([(knew_ref, vnew_ref)], qidx > kidx, jnp.zeros(carry_ref.shape, F32), jnp.zeros(acc_ref.shape, F32))

    run(list(zip(k_refs, v_refs)), None, carry_ref[...], acc_ref[...])

    @pl.when(s == pl.num_programs(1) - 1)
    def _():
        row_head = lax.broadcasted_iota(jnp.int32, acc_ref.shape, 0) >> _log2(SB_QROWS)
        col_head = lax.broadcasted_iota(jnp.int32, acc_ref.shape, 1) >> _log2(SB_HEAD_DIM)
        own = jnp.where(row_head == col_head, acc_ref[...], 0.0)
        o_ref[0] = sum(own[:, h * SB_HEAD_DIM:(h + 1) * SB_HEAD_DIM] for h in range(SB_HEADS))


def sb_sample(q, k_new, v_new, bias, cache_k, cache_v, layer, page_table):
    DB, t_new, _ = q.shape
    assert t_new <= SB_QROWS
    n_pages = page_table.shape[1]
    pages_per_step = math.gcd(SB_PAGES_PER_STEP, n_pages)
    rows = SB_HEADS * SB_QROWS
    heads = lambda a: a.reshape(DB, t_new, SB_HEADS, SB_HEAD_DIM)
    q_hm = jnp.pad(heads(q * SB_SCALE).transpose(0, 2, 1, 3), ((0, 0), (0, 0), (0, SB_QROWS - t_new), (0, 0)))
    own = jnp.eye(SB_HEADS, dtype=F32)[None, :, None, :, None]
    q_blk = (q_hm[:, :, :, None, :] * own).reshape(DB, rows, W_MIX).astype(BF16)
    bias_col = jnp.broadcast_to(jnp.repeat(bias, SB_QROWS)[:, None], (rows, SB_BLOCK)).astype(F32)
    as_page = lambda a: jnp.pad(heads(a).transpose(0, 2, 3, 1), ((0, 0), (0, 0), (0, 0), (0, SB_BLOCK - t_new)))[:, None]

    def page_spec(j):
        def index(b, s, pt):
            return (layer, pt[b, n_pages - 1 - (s * pages_per_step + j)], 0, 0, 0)
        return pl.BlockSpec((1, 1, SB_HEADS, SB_HEAD_DIM, SB_BLOCK), index)

    per_seq = lambda shape: pl.BlockSpec((1,) + shape, lambda b, s, pt: (b,) + (0,) * len(shape))
    new_page = per_seq((1, SB_HEADS, SB_HEAD_DIM, SB_BLOCK))
    grid_spec = pltpu.PrefetchScalarGridSpec(
        num_scalar_prefetch=1,
        grid=(DB, n_pages // pages_per_step),
        in_specs=[per_seq((rows, W_MIX)), pl.BlockSpec((rows, SB_BLOCK), lambda b, s, pt: (0, 0)), new_page, new_page]
        + [page_spec(j) for j in range(pages_per_step)] * 2,
        out_specs=per_seq((rows, SB_HEAD_DIM)),
        scratch_shapes=[pltpu.VMEM((rows, SB_BLOCK), F32), pltpu.VMEM((rows, W_MIX), F32)],
    )
    o = pl.pallas_call(
        functools.partial(_sb_sample_kernel, pages_per_step=pages_per_step),
        grid_spec=grid_spec,
        out_shape=jax.ShapeDtypeStruct((DB, rows, SB_HEAD_DIM), F32),
        compiler_params=_params("parallel", "arbitrary"),
        name="sb_sample",
    )(page_table, q_blk, bias_col, as_page(k_new), as_page(v_new), *([cache_k] * pages_per_step),
      *([cache_v] * pages_per_step))
    o = o.reshape(DB, SB_HEADS, SB_QROWS, SB_HEAD_DIM)[:, :, :t_new]
    return o.transpose(0, 2, 1, 3).reshape(DB, t_new, W_MIX)


def _merge_kernel(x_ref, ya_ref, yb_ref, yc_ref, yd_ref, gate_ref, bg_ref, wbr_ref, wout_ref, o_ref, *, split):
    D = x_ref.shape[-1]
    merged = jnp.zeros(x_ref.shape, F32)
    for i, y_ref in enumerate((ya_ref, yb_ref, yc_ref, yd_ref)):
        gate = _sigmoid(gate_ref[:, i * D:(i + 1) * D] + bg_ref[i:i + 1, :])
        merged = merged + gate * _mm(y_ref[...], wbr_ref[i], split)
    o_ref[...] = x_ref[...] + _mm(merged, wout_ref[...], split)


def merge_out(x, ya, yb, yc, yd, proj, b_gate, w_br, w_out, tm, split=False):
    M, D = x.shape
    act = pl.BlockSpec((tm, W_MIX), lambda i: (i, 0))
    whole = lambda a: pl.BlockSpec(a.shape, lambda i: (0,) * a.ndim)
    return pl.pallas_call(
        functools.partial(_merge_kernel, split=split),
        grid=(M // tm,),
        in_specs=[pl.BlockSpec((tm, D), lambda i: (i, 0)), act, act, act, act,
                  pl.BlockSpec((tm, 4 * D), lambda i: (i, 1)), whole(b_gate), whole(w_br), whole(w_out)],
        out_specs=pl.BlockSpec((tm, D), lambda i: (i, 0)),
        out_shape=jax.ShapeDtypeStruct((M, D), F32),
        compiler_params=_params("parallel"),
        name="merge_out",
    )(x, ya, yb, yc, yd, proj, b_gate, w_br, w_out)


def _mem_attn_kernel(q_ref, k_ref, v_ref, o_ref, *, split):
    dh = q_ref.shape[-1] // MEM_HEADS
    scale = dh ** -0.5
    for h in range(MEM_HEADS):
        cols = slice(h * dh, (h + 1) * dh)
        s = _mm(q_ref[0, :, cols], k_ref[0, :, cols], split, nt=True) * scale
        e = jnp.exp(s - jnp.max(s, axis=-1, keepdims=True))
        p = e / jnp.sum(e, axis=-1, keepdims=True)
        o_ref[0, :, cols] = _mm(p, v_ref[0, :, cols], split).astype(o_ref.dtype)


def mem_attn(q, mem_k, mem_v, tt, act_dtype, split=False):
    B, T, D = q.shape
    n_mem = mem_k.shape[1]
    mem = pl.BlockSpec((1, n_mem, D), lambda b, t: (b, 0, 0))
    return pl.pallas_call(
        functools.partial(_mem_attn_kernel, split=split),
        grid=(B, T // tt),
        in_specs=[pl.BlockSpec((1, tt, D), lambda b, t: (b, t, 0)), mem, mem],
        out_specs=pl.BlockSpec((1, tt, D), lambda b, t: (b, t, 0)),
        out_shape=jax.ShapeDtypeStruct((B, T, D), act_dtype),
        compiler_params=_params("parallel", "arbitrary"),
        name="mem_attn",
    )(q, mem_k, mem_v)


def _route(logits):
    lane_i = lax.broadcasted_iota(jnp.int32, logits.shape, 1)
    lane = lane_i.astype(F32)
    neg = -jnp.inf
    big = float(LANES)
    lg = jnp.where(lane_i < N_GROUPS, logits, neg)
    g_max = jnp.max(lg, axis=-1, keepdims=True)
    g_w = 1.0 / jnp.sum(jnp.exp(lg - g_max), axis=-1, keepdims=True)
    g_idx = jnp.min(jnp.where(lg == g_max, lane, big), axis=-1, keepdims=True)
    e_lane = lane_i - N_GROUPS
    e_group = (e_lane >> _log2(EXPERTS_PER_GROUP)).astype(F32)
    in_group = (e_lane >= 0) & (e_lane < N_EXPERTS) & (e_group == g_idx)
    le = jnp.where(in_group, logits, neg)
    v1 = jnp.max(le, axis=-1, keepdims=True)
    i1 = jnp.min(jnp.where(in_group & (le == v1), lane, big), axis=-1, keepdims=True)
    rest = in_group & (lane != i1)
    le2 = jnp.where(rest, logits, neg)
    v2 = jnp.max(le2, axis=-1, keepdims=True)
    i2 = jnp.min(jnp.where(rest & (le2 == v2), lane, big), axis=-1, keepdims=True)
    e2 = jnp.exp(v2 - v1)
    w1 = 1.0 / (1.0 + e2)
    w2 = e2 / (1.0 + e2)
    return g_w * (jnp.where(lane == i1, w1, 0.0) + jnp.where(lane == i2, w2, 0.0))


def _moe_kernel(x_ref, g_ref, wr_ref, br_ref, w13_ref, w2_ref, gf_ref, o_ref, h_ref, comb_ref, acc_ref, *, final_norm,
                split):
    e = pl.program_id(1)

    @pl.when(e == 0)
    def _():
        h = _rms(x_ref[...], g_ref[...])
        h_ref[...] = h.astype(h_ref.dtype)
        comb_ref[...] = _route(_mm(h, wr_ref[...], split) + br_ref[...])
        acc_ref[...] = jnp.zeros_like(acc_ref)

    de = w2_ref.shape[1]
    lane = lax.broadcasted_iota(jnp.int32, comb_ref.shape, 1)
    weight = jnp.sum(jnp.where(lane == e + N_GROUPS, comb_ref[...], 0.0), axis=-1, keepdims=True)
    hid13 = _mm(h_ref[...], w13_ref[0], split)
    g1 = hid13[:, :de]
    hid = (g1 * _sigmoid(g1)) * hid13[:, de:]
    acc_ref[...] += weight * _mm(hid, w2_ref[0], split)

    @pl.when(e == pl.num_programs(1) - 1)
    def _():
        y = x_ref[...] + acc_ref[...]
        o_ref[...] = _rms(y, gf_ref[...]) if final_norm else y


def moe(x, g, w_router, b_router, w13, w2, g_final, tm, final_norm, split=False):
    M, D = x.shape
    de = w2.shape[1]
    vec = lambda a: pl.BlockSpec(a.shape, lambda i, e: (0,) * a.ndim)
    g, g_final = g.reshape(1, D), g_final.reshape(1, D)
    return pl.pallas_call(
        functools.partial(_moe_kernel, final_norm=final_norm, split=split),
        grid=(M // tm, N_EXPERTS),
        in_specs=[pl.BlockSpec((tm, D), lambda i, e: (i, 0)), vec(g), vec(w_router), vec(b_router),
                  pl.BlockSpec((1, D, 2 * de), lambda i, e: (e, 0, 0)),
                  pl.BlockSpec((1, de, D), lambda i, e: (e, 0, 0)), vec(g_final)],
        out_specs=pl.BlockSpec((tm, D), lambda i, e: (i, 0)),
        out_shape=jax.ShapeDtypeStruct((M, D), F32),
        scratch_shapes=[pltpu.VMEM((tm, D), F32 if split else BF16), pltpu.VMEM((tm, LANES), F32),
                        pltpu.VMEM((tm, D), F32)],
        compiler_params=_params("parallel", "arbitrary"),
        name="moe",
    )(x, g, w_router, b_router, w13, w2, g_final)


def _layer(x, lw, pool_prev, conv_prev, mem_k, mem_v, sb_fn, *, start_pos, tiles, act_dtype, final_norm, split):
    B, T, D = x.shape
    M = B * T
    x2 = x.reshape(M, D)
    proj = norm_matmul(x2, lw["g_mix"], lw["w_in"], tiles["tm_in"], tiles["tn_in"], split)
    proj3 = proj.reshape(B, T, -1)
    ya, yb, yd, vn, pool_state, conv_state = mixers(
        proj3, pool_prev, conv_prev, lw["w_pool_map"], lw["pool_scale"], lw["g_sgu_v"], lw["w_sgu_s"], lw["b_sgu"],
        lw["w_dw"], lw["b_dw"], lw["ln_g"], lw["ln_b"], tt=tiles["tt_mix"], start_pos=start_pos, act_dtype=act_dtype,
        split=split)
    k_new = proj3[:, :, 4 * W_MIX:5 * W_MIX]
    v_new = proj3[:, :, 5 * W_MIX:6 * W_MIX]
    yc = sb_fn(proj3, k_new, v_new)
    flat = lambda a: a.reshape(M, W_MIX)
    x2 = merge_out(x2, flat(ya), flat(yb), flat(yc), flat(yd), proj, lw["b_gate"], lw["w_br"], lw["w_out"],
                   tiles["tm_merge"], split)
    q = norm_matmul(x2, lw["g_xattn"], lw["w_xq"], tiles["tm_in"], tiles["tn_in"], split)
    o = mem_attn(q.reshape(B, T, D), mem_k, mem_v, tiles["tt_mem"], act_dtype, split)
    x2 = matmul_residual(x2, o.reshape(M, D), lw["w_xo"], tiles["tm_merge"], split)
    x2 = moe(x2, lw["g_moe"], lw["w_router"], lw["b_router"], lw["w_e13"], lw["w_e2"], lw["g_final"], tiles["tm_moe"],
             final_norm, split)
    heads = lambda a: a.reshape(B, T, SB_HEADS, SB_HEAD_DIM)
    return x2.reshape(B, T, D), pool_state, conv_state, heads(k_new), heads(v_new), vn


def kernel(x_prompt, x_sample, mem_prompt, cache_k, cache_v, cache_mem_k, cache_mem_v, state_pool, state_conv,
           page_table, g_mix, w_in, b_gate, w_pool_map, pool_scale, w_br_pool, g_sgu_v, w_sgu_s, b_sgu, w_br_sgu,
           sb_bias, w_br_sb, w_dw, b_dw, ln_g, ln_b, w_br_conv, w_out, g_xattn, g_mem, w_xq, w_xk, w_xv, w_xo,
           g_moe, w_rg, b_rg, w_re, b_re, w_e1, w_e3, w_e2, g_final):
    B, T, D = x_prompt.shape
    DB, TS, _ = x_sample.shape
    depth = w_in.shape[0]
    n_mem = mem_prompt.shape[1]
    past_len = page_table.shape[1] * cache_k.shape[2]
    cache_kt = cache_k.transpose(0, 1, 3, 4, 2)
    cache_vt = cache_v.transpose(0, 1, 3, 4, 2)
    pad = LANES - N_GROUPS - N_EXPERTS

    prompt_tiles = dict(tm_in=min(B * T, 1024), tn_in=1024, tt_mix=min(T, 256), tm_merge=min(B * T, 256),
                        tt_mem=min(T, 512), tm_moe=min(B * T, 1024))
    sample_tiles = dict(tm_in=DB * TS, tn_in=1024, tt_mix=TS, tm_merge=DB * TS, tt_mem=TS, tm_moe=DB * TS)

    xp, xs = x_prompt, x_sample
    outs = [[] for _ in range(11)]
    for l in range(depth):
        lw32 = dict(
            g_mix=g_mix[l], w_in=w_in[l], b_gate=b_gate[l], w_pool_map=w_pool_map[l],
            pool_scale=pool_scale[l], g_sgu_v=g_sgu_v[l], w_sgu_s=w_sgu_s[l], b_sgu=b_sgu[l],
            w_dw=w_dw[l], b_dw=b_dw[l], ln_g=ln_g[l], ln_b=ln_b[l],
            w_br=jnp.stack([w_br_pool[l], w_br_sgu[l], w_br_sb[l], w_br_conv[l]]),
            w_out=w_out[l], g_xattn=g_xattn[l], w_xq=w_xq[l], w_xo=w_xo[l], g_moe=g_moe[l],
            w_router=jnp.pad(jnp.concatenate([w_rg[l], w_re[l]], axis=1), ((0, 0), (0, pad))),
            b_router=jnp.pad(jnp.concatenate([b_rg[l], b_re[l]]), (0, pad)).reshape(1, LANES),
            w_e13=jnp.concatenate([w_e1[l], w_e3[l]], axis=-1), w_e2=w_e2[l], g_final=g_final)
        mxu_weights = ("w_in", "w_pool_map", "w_br", "w_out", "w_xq", "w_xo", "w_router", "w_e13", "w_e2")
        lw = {k: v.astype(BF16) if k in mxu_weights else v for k, v in lw32.items()}
        final_norm = l == depth - 1

        w_kv = jnp.concatenate([w_xk[l], w_xv[l]], axis=1).astype(BF16)
        mkv = norm_matmul(mem_prompt.reshape(B * n_mem, D), g_mem[l], w_kv, min(B * n_mem, 1024), 1024)
        mk = mkv[:, :D].reshape(B, n_mem, D)
        mv = mkv[:, D:].reshape(B, n_mem, D)
        sb_p = lambda proj3, k_new, v_new: sb_prompt(proj3, sb_bias[l], BF16)
        xp, pool_p, conv_p, k_p, v_p, _ = _layer(
            xp, lw, jnp.zeros((B, POOL_STATE, W_MIX), F32), jnp.zeros((B, CONV_STATE, W_MIX), F32), mk, mv, sb_p,
            start_pos=0, tiles=prompt_tiles, act_dtype=BF16, final_norm=final_norm, split=False)

        sb_s = lambda proj3, k_new, v_new: sb_sample(
            proj3[:, :, 3 * W_MIX:4 * W_MIX], k_new, v_new, sb_bias[l], cache_kt, cache_vt, l, page_table)
        xs, pool_s, conv_s, k_s, v_s, vn_s = _layer(
            xs, lw32, state_pool[l], state_conv[l], cache_mem_k[l].reshape(DB, n_mem, D),
            cache_mem_v[l].reshape(DB, n_mem, D), sb_s, start_pos=past_len, tiles=sample_tiles, act_dtype=F32,
            final_norm=final_norm, split=True)

        heads = lambda a: a.reshape(B, n_mem, MEM_HEADS, D // MEM_HEADS)
        for lst, val in zip(outs, (k_p, v_p, k_s, v_s, pool_p, pool_s, conv_p, conv_s, vn_s, heads(mk), heads(mv))):
            lst.append(val)
    return (xp, xs) + tuple(jnp.stack(o) for o in outs)
```
